```python
import math
import jax, jax.numpy as jnp
from jax import lax
import numpy as np

D_MODEL = 4096
BATCH = 32
SEQ = 256
DEPTH = 4
DEC_BATCH = 8
DEC_SEQ = 2048
PAST_LEN = 256

GRID_W = 64
N_EVEN = (DEPTH + 1) // 2
N_ODD = DEPTH // 2
HEAD_DIM = 128
H_A = D_MODEL // (2 * HEAD_DIM)
H_B = D_MODEL // (2 * HEAD_DIM)
DQK_A = HEAD_DIM // 2
AB_WIDTH = (H_A + H_B) * HEAD_DIM
WIN_ROWS = 8
WIN_COLS = 16
H_C = D_MODEL // HEAD_DIM
Q_LORA = D_MODEL // 4
KV_LORA = D_MODEL // 8
NOPE_DIM = 128
ROPE_DIM = 64
V_DIM = 128
N_GROUPS = 4
EXPERTS_PER_GROUP = 8
D_EXPERT = D_MODEL // 8
TOP_K_IN_GROUP = 2
ROPE_THETA = 10000.0
NORM_EPS = 1e-6
Q_BLOCK = 128
NEG_INF = -1e30

kernel_name = 'hybrid_diffusion_prefix_trunk_step'


def _rmsnorm(x, g):
    xf = x.astype(jnp.float32)
    y = xf * lax.rsqrt(jnp.mean(xf * xf, axis=-1, keepdims=True) + NORM_EPS)
    return (y * g.astype(jnp.float32)).astype(x.dtype)


def _modulation(cvec, w, b):
    return jnp.split(jax.nn.silu(cvec) @ w + b, 6, axis=-1)


def _modulate(x, g, shift, scale):
    return _rmsnorm(x, g) * (1.0 + scale) + shift


def _axial_rope(x):
    S, dim = x.shape[1], x.shape[-1]
    half = dim // 2
    quarter = half // 2
    t = jnp.arange(S)
    inv = ROPE_THETA ** (-jnp.arange(0, half, 2, dtype=jnp.float32) / half)
    ang_r = (t // GRID_W).astype(jnp.float32)[:, None] * inv
    ang_c = (t % GRID_W).astype(jnp.float32)[:, None] * inv
    ang = jnp.concatenate([ang_r, ang_r, ang_c, ang_c], axis=-1)
    ang = ang.reshape((S,) + (1,) * (x.ndim - 3) + (dim,))
    xf = x.astype(jnp.float32)
    x1, x2 = xf[..., :quarter], xf[..., quarter:half]
    x3, x4 = xf[..., half:half + quarter], xf[..., half + quarter:]
    rot = jnp.concatenate([-x2, x1, -x4, x3], axis=-1)
    return (xf * jnp.cos(ang) + rot * jnp.sin(ang)).astype(x.dtype)


def _map_query_blocks(fn, qs):
    B, S = qs[0].shape[:2]
    nb = S // Q_BLOCK
    blocks = tuple(jnp.moveaxis(a.reshape((B, nb, Q_BLOCK) + a.shape[2:]), 1, 0) for a in qs)
    out = lax.map(fn, blocks)
    return jnp.moveaxis(out, 0, 1).reshape((B, S) + out.shape[3:])


def _ab_project(h, w_in):
    B, S, _ = h.shape
    wa, wb = H_A * HEAD_DIM, H_B * HEAD_DIM
    qa, ka, va, qb, kb, vb = jnp.split(h @ w_in, [wa, 2 * wa, 3 * wa, 3 * wa + wb, 3 * wa + 2 * wb], axis=-1)
    qa = qa.reshape(B, S, H_A, 2, DQK_A)
    ka = ka.reshape(B, S, H_A, 2, DQK_A)
    va = va.reshape(B, S, H_A, HEAD_DIM)
    qb = qb.reshape(B, S, H_B, HEAD_DIM)
    kb = kb.reshape(B, S, H_B, HEAD_DIM)
    vb = vb.reshape(B, S, H_B, HEAD_DIM)
    return qa, ka, va, qb, kb, vb


def _diff_lambda(lq1, lk1, lq2, lk2, lam_init):
    return (jnp.exp(jnp.sum((lq1 * lk1).astype(jnp.float32)))
            - jnp.exp(jnp.sum((lq2 * lk2).astype(jnp.float32))) + lam_init)


def _diff_attn(q, k, v, lam, lam_init, g_sub):
    scale = DQK_A ** -0.5

    def block(args):
        (qb,) = args
        s = jnp.einsum('bqhcd,bkhcd->cbhqk', qb, k).astype(jnp.float32) * scale
        p = jax.nn.softmax(s, axis=-1)
        a = p[0] - lam * p[1]
        return jnp.einsum('bhqk,bkhd->bqhd', a.astype(v.dtype), v)

    o = _map_query_blocks(block, (q,))
    o = _rmsnorm(o, g_sub) * (1.0 - lam_init)
    return o.reshape(o.shape[:2] + (-1,))


def _softmax_attn(q, k, v):
    scale = q.shape[-1] ** -0.5

    def block(args):
        (qb,) = args
        s = jnp.einsum('bqhd,bkhd->bhqk', qb, k).astype(jnp.float32) * scale
        p = jax.nn.softmax(s, axis=-1).astype(v.dtype)
        return jnp.einsum('bhqk,bkhd->bqhd', p, v)

    o = _map_query_blocks(block, (q,))
    return o.reshape(o.shape[:2] + (-1,))


def _neighbourhood_attn(q, k, v, k_ctx, v_ctx, rpb):
    B, S, H, D = q.shape
    rows = S // GRID_W
    kr = min(WIN_ROWS, rows)
    kc = min(WIN_COLS, GRID_W)
    scale = D ** -0.5
    kg = k.reshape(B, rows, GRID_W, H, D)
    vg = v.reshape(B, rows, GRID_W, H, D)
    qg = jnp.moveaxis(q.reshape(B, rows, GRID_W, H, D), 1, 0)
    cols = jnp.arange(GRID_W)
    col_start = jnp.clip(cols - kc // 2, 0, GRID_W - kc)
    col_ok = (cols[None, :] >= col_start[:, None]) & (cols[None, :] < col_start[:, None] + kc)
    mask = jnp.broadcast_to(col_ok[:, None, :], (GRID_W, kr, GRID_W)).reshape(GRID_W, kr * GRID_W)
    col_idx = jnp.clip(cols[None, :] - cols[:, None], -(WIN_COLS - 1), WIN_COLS - 1) + WIN_COLS - 1
    n_loc = kr * GRID_W

    def row_block(args):
        q_r, r = args
        r0 = jnp.clip(r - kr // 2, 0, rows - kr)
        k_w = lax.dynamic_slice_in_dim(kg, r0, kr, axis=1).reshape(B, n_loc, H, D)
        v_w = lax.dynamic_slice_in_dim(vg, r0, kr, axis=1).reshape(B, n_loc, H, D)
        row_idx = r0 + jnp.arange(kr) - r + WIN_ROWS - 1
        bias = rpb[:, row_idx[:, None, None], col_idx[None, :, :]]
        bias = jnp.transpose(bias, (0, 2, 1, 3)).reshape(H, GRID_W, n_loc).astype(jnp.float32)
        s_loc = jnp.einsum('bqhd,bkhd->bhqk', q_r, k_w).astype(jnp.float32) * scale + bias
        s_loc = jnp.where(mask, s_loc, NEG_INF)
        s_ctx = jnp.einsum('bqhd,bkhd->bhqk', q_r, k_ctx).astype(jnp.float32) * scale
        p = jax.nn.softmax(jnp.concatenate([s_loc, s_ctx], axis=-1), axis=-1).astype(v.dtype)
        return (jnp.einsum('bhqk,bkhd->bqhd', p[..., :n_loc], v_w)
                + jnp.einsum('bhqk,bkhd->bqhd', p[..., n_loc:], v_ctx))

    o = lax.map(row_block, (qg, jnp.arange(rows)))
    return jnp.moveaxis(o, 0, 1).reshape(B, S, H * D)


def _mla_project(h, w_down, g_q, g_kv, w_uq):
    B, S, _ = h.shape
    cq, ckv, kr = jnp.split(h @ w_down, [Q_LORA, Q_LORA + KV_LORA], axis=-1)
    q = (_rmsnorm(cq, g_q) @ w_uq).reshape(B, S, H_C, NOPE_DIM + ROPE_DIM)
    return q[..., :NOPE_DIM], q[..., NOPE_DIM:], _rmsnorm(ckv, g_kv), kr


def _mla_attn(q_nope, q_rope, ckv, kr, w_uk, w_uv):
    B, Sk = ckv.shape[:2]
    k_nope = (ckv @ w_uk).reshape(B, Sk, H_C, NOPE_DIM)
    v = (ckv @ w_uv).reshape(B, Sk, H_C, V_DIM)
    scale = (NOPE_DIM + ROPE_DIM) ** -0.5

    def block(args):
        qn, qr = args
        s = (jnp.einsum('bqhd,bkhd->bhqk', qn, k_nope)
             + jnp.einsum('bqhd,bkd->bhqk', qr, kr)).astype(jnp.float32) * scale
        p = jax.nn.softmax(s, axis=-1).astype(v.dtype)
        return jnp.einsum('bhqk,bkhd->bqhd', p, v)

    o = _map_query_blocks(block, (q_nope, q_rope))
    return o.reshape(o.shape[:2] + (-1,))


def _hier_moe(h, w_gr, b_gr, w_er, b_er, w1, w3, w2):
    shp = h.shape
    hf = h.reshape(-1, shp[-1])
    g_logits = (hf @ w_gr).astype(jnp.float32) + b_gr.astype(jnp.float32)
    p_group = jax.nn.softmax(g_logits, axis=-1)
    _, g_top = lax.top_k(g_logits, 1)
    g_idx = g_top[:, 0]
    e_logits = ((hf @ w_er).astype(jnp.float32) + b_er.astype(jnp.float32)).reshape(-1, N_GROUPS, EXPERTS_PER_GROUP)
    e_sel = jnp.take_along_axis(e_logits, g_idx[:, None, None], axis=1)[:, 0]
    top_v, top_i = lax.top_k(e_sel, TOP_K_IN_GROUP)
    sel_w = jax.nn.softmax(top_v, axis=-1) * jnp.take_along_axis(p_group, g_idx[:, None], axis=1)
    in_group = jnp.sum(jax.nn.one_hot(top_i, EXPERTS_PER_GROUP, dtype=jnp.float32) * sel_w[..., None], axis=1)
    comb = jax.nn.one_hot(g_idx, N_GROUPS, dtype=jnp.float32)[:, :, None] * in_group[:, None, :]
    out = jnp.zeros_like(hf)
    for g in range(N_GROUPS):
        a = jnp.einsum('td,edf->tef', hf, w1[g])
        b = jnp.einsum('td,edf->tef', hf, w3[g])
        hid = jax.nn.silu(a) * b * comb[:, g, :, None].astype(hf.dtype)
        out = out + jnp.einsum('tef,efd->td', hid, w2[g])
    return out.reshape(shp)


def setup_inputs(seed: int = 0) -> dict:
    key = jax.random.key(seed)
    ks = iter(jax.random.split(key, 64))

    def nrm(shape, scale=1.0):
        return jax.random.normal(next(ks), shape, jnp.float32) * scale

    D = D_MODEL
    return {
        'x_prompt': nrm((BATCH, SEQ, D)),
        'x_sample': nrm((DEC_BATCH, DEC_SEQ, D)),
        'c': nrm((DEC_BATCH, D)),
        'c_ctx': nrm((D,)),
        'cache_a_k': nrm((DEC_BATCH, N_EVEN, PAST_LEN, H_A, HEAD_DIM)),
        'cache_a_v': nrm((DEC_BATCH, N_EVEN, PAST_LEN, H_A, HEAD_DIM)),
        'cache_b_k': nrm((DEC_BATCH, N_EVEN, PAST_LEN, H_B, HEAD_DIM)),
        'cache_b_v': nrm((DEC_BATCH, N_EVEN, PAST_LEN, H_B, HEAD_DIM)),
        'cache_c_kv': nrm((DEC_BATCH, N_ODD, PAST_LEN, KV_LORA)),
        'cache_c_kr': nrm((DEC_BATCH, N_ODD, PAST_LEN, ROPE_DIM)),
        'g_norm1': 1.0 + nrm((DEPTH, D), 0.02),
        'g_norm2': 1.0 + nrm((DEPTH, D), 0.02),
        'g_final': 1.0 + nrm((D,), 0.02),
        'w_ada': nrm((DEPTH, D, 6 * D), 0.5 * D ** -0.5),
        'b_ada': nrm((DEPTH, 6 * D), 0.02),
        'w_in_ab': nrm((N_EVEN, D, 3 * AB_WIDTH), D ** -0.5),
        'w_out_ab': nrm((N_EVEN, AB_WIDTH, D), AB_WIDTH ** -0.5),
        'lam_q1': nrm((N_EVEN, DQK_A), 0.1),
        'lam_k1': nrm((N_EVEN, DQK_A), 0.1),
        'lam_q2': nrm((N_EVEN, DQK_A), 0.1),
        'lam_k2': nrm((N_EVEN, DQK_A), 0.1),
        'g_sub_a': 1.0 + nrm((N_EVEN, HEAD_DIM), 0.02),
        'rpb_b': nrm((N_EVEN, H_B, 2 * WIN_ROWS - 1, 2 * WIN_COLS - 1), 0.1),
        'w_down_c': nrm((N_ODD, D, Q_LORA + KV_LORA + ROPE_DIM), D ** -0.5),
        'g_q_c': 1.0 + nrm((N_ODD, Q_LORA), 0.02),
        'g_kv_c': 1.0 + nrm((N_ODD, KV_LORA), 0.02),
        'w_uq_c': nrm((N_ODD, Q_LORA, H_C * (NOPE_DIM + ROPE_DIM)), Q_LORA ** -0.5),
        'w_uk_c': nrm((N_ODD, KV_LORA, H_C * NOPE_DIM), KV_LORA ** -0.5),
        'w_uv_c': nrm((N_ODD, KV_LORA, H_C * V_DIM), KV_LORA ** -0.5),
        'w_out_c': nrm((N_ODD, H_C * V_DIM, D), (H_C * V_DIM) ** -0.5),
        'w_group_router': nrm((DEPTH, D, N_GROUPS), D ** -0.5),
        'b_group_router': nrm((DEPTH, N_GROUPS), 0.01),
        'w_expert_router': nrm((DEPTH, D, N_GROUPS * EXPERTS_PER_GROUP), D ** -0.5),
        'b_expert_router': nrm((DEPTH, N_GROUPS * EXPERTS_PER_GROUP), 0.01),
        'w1_moe': nrm((DEPTH, N_GROUPS, EXPERTS_PER_GROUP, D, D_EXPERT), D ** -0.5),
        'w3_moe': nrm((DEPTH, N_GROUPS, EXPERTS_PER_GROUP, D, D_EXPERT), D ** -0.5),
        'w2_moe': nrm((DEPTH, N_GROUPS, EXPERTS_PER_GROUP, D_EXPERT, D), D_EXPERT ** -0.5),
    }


def reference(x_prompt, x_sample, c, c_ctx, cache_a_k, cache_a_v, cache_b_k, cache_b_v, cache_c_kv, cache_c_kr,
              g_norm1, g_norm2, g_final, w_ada, b_ada, w_in_ab, w_out_ab, lam_q1, lam_k1, lam_q2, lam_k2,
              g_sub_a, rpb_b, w_down_c, g_q_c, g_kv_c, w_uq_c, w_uk_c, w_uv_c, w_out_c,
              w_group_router, b_group_router, w_expert_router, b_expert_router, w1_moe, w3_moe, w2_moe):
    xc, xs = x_prompt, x_sample
    bc, n_ctx = xc.shape[:2]
    bs = xs.shape[0]
    sak, sav, sbk, sbv, sckv, sckr = [], [], [], [], [], []
    for l in range(DEPTH):
        sh1c, sc1c, gt1c, sh2c, sc2c, gt2c = _modulation(c_ctx, w_ada[l], b_ada[l])
        sh1s, sc1s, gt1s, sh2s, sc2s, gt2s = [m[:, None, :] for m in _modulation(c, w_ada[l], b_ada[l])]
        hc = _modulate(xc, g_norm1[l], sh1c, sc1c)
        hs = _modulate(xs, g_norm1[l], sh1s, sc1s)
        if l % 2 == 0:
            e = l // 2
            lam_init = 0.8 - 0.6 * math.exp(-0.3 * l)
            lam = _diff_lambda(lam_q1[e], lam_k1[e], lam_q2[e], lam_k2[e], lam_init)
            qa, ka, va, qb, kb, vb = _ab_project(hc, w_in_ab[e])
            oa = _diff_attn(qa, ka, va, lam, lam_init, g_sub_a[e])
            ob = _softmax_attn(qb, kb, vb)
            mix_c = jnp.concatenate([oa, ob], axis=-1) @ w_out_ab[e]
            sak.append(ka.reshape(bc, n_ctx, H_A, HEAD_DIM))
            sav.append(va)
            sbk.append(kb)
            sbv.append(vb)
            qa, ka, va, qb, kb, vb = _ab_project(hs, w_in_ab[e])
            ka_all = jnp.concatenate([_axial_rope(ka), cache_a_k[:, e].reshape(bs, -1, H_A, 2, DQK_A)], axis=1)
            va_all = jnp.concatenate([va, cache_a_v[:, e]], axis=1)
            oa = _diff_attn(_axial_rope(qa), ka_all, va_all, lam, lam_init, g_sub_a[e])
            ob = _neighbourhood_attn(qb, kb, vb, cache_b_k[:, e], cache_b_v[:, e], rpb_b[e])
            mix_s = jnp.concatenate([oa, ob], axis=-1) @ w_out_ab[e]
        else:
            oi = l // 2
            qn, qr, ckv, kr = _mla_project(hc, w_down_c[oi], g_q_c[oi], g_kv_c[oi], w_uq_c[oi])
            mix_c = _mla_attn(qn, qr, ckv, kr, w_uk_c[oi], w_uv_c[oi]) @ w_out_c[oi]
            sckv.append(ckv)
            sckr.append(kr)
            qn, qr, ckv, kr = _mla_project(hs, w_down_c[oi], g_q_c[oi], g_kv_c[oi], w_uq_c[oi])
            ckv_all = jnp.concatenate([ckv, cache_c_kv[:, oi]], axis=1)
            kr_all = jnp.concatenate([_axial_rope(kr), cache_c_kr[:, oi]], axis=1)
            mix_s = _mla_attn(qn, _axial_rope(qr), ckv_all, kr_all, w_uk_c[oi], w_uv_c[oi]) @ w_out_c[oi]
        xc = xc + gt1c * mix_c
        xs = xs + gt1s * mix_s
        xc = xc + gt2c * _hier_moe(_modulate(xc, g_norm2[l], sh2c, sc2c), w_group_router[l], b_group_router[l],
                                   w_expert_router[l], b_expert_router[l], w1_moe[l], w3_moe[l], w2_moe[l])
        xs = xs + gt2s * _hier_moe(_modulate(xs, g_norm2[l], sh2s, sc2s), w_group_router[l], b_group_router[l],
                                   w_expert_router[l], b_expert_router[l], w1_moe[l], w3_moe[l], w2_moe[l])
    y_prompt = _rmsnorm(xc, g_final)
    y_sample = _rmsnorm(xs, g_final)
    state_a_k = jnp.stack(sak, axis=1)
    state_a_v = jnp.stack(sav, axis=1)
    state_b_k = jnp.stack(sbk, axis=1)
    state_b_v = jnp.stack(sbv, axis=1)
    state_c_kv = jnp.stack(sckv, axis=1)
    state_c_kr = jnp.stack(sckr, axis=1)
    return (y_prompt, y_sample, state_a_k, state_a_v, state_b_k, state_b_v, state_c_kv, state_c_kr)
```

```python
import functools
import math

import numpy as np
import jax
import jax.numpy as jnp
from jax import lax
from jax.experimental import pallas as pl
from jax.experimental.pallas import tpu as pltpu

GRID_W = 64
HEAD_DIM = 128
WIN_ROWS = 8
WIN_COLS = 16
NOPE_DIM = 128
ROPE_DIM = 64
V_DIM = 128
N_GROUPS = 4
EXPERTS_PER_GROUP = 8
ROPE_THETA = 10000.0
NORM_EPS = 1e-6
NEG_INF = -1e30

LANES = 128
COND_ROWS_PAD = 16
VMEM_LIMIT = 56 * 1024 * 1024
ROW_TILE = 512
COL_TILE = 512
NORM_TILE = 256
MOE_TILE = 256
Q_TILE = 256
NBR_ROWS = 4

f32 = jnp.float32
bf16 = jnp.bfloat16


def _cp(*sem):
    return pltpu.CompilerParams(dimension_semantics=sem, vmem_limit_bytes=VMEM_LIMIT)


def _pick(n, pref):
    if n <= pref:
        return n
    t = pref
    while n % t:
        t //= 2
    return t


def _cond_row(r0, tc, s_lat):
    return jnp.where(r0 < tc, 0, 1 + (r0 - tc) // s_lat)


def _dot_nt(a, b):
    return lax.dot_general(a, b, (((1,), (1,)), ((), ())), preferred_element_type=f32)


def _ada_kernel(c_ref, w_ref, b_ref, o_ref):
    s = jax.nn.silu(c_ref[...]).astype(bf16)
    o_ref[...] = jnp.dot(s, w_ref[...].astype(bf16), preferred_element_type=f32) + b_ref[...]


def _ada_mod(cond, w_ada, b_ada):
    depth, d, n = w_ada.shape
    r = cond.shape[0]
    tn = _pick(n, COL_TILE)
    return pl.pallas_call(
        _ada_kernel,
        grid=(depth, n // tn),
        in_specs=[pl.BlockSpec((r, d), lambda l, j: (0, 0)),
                  pl.BlockSpec((None, d, tn), lambda l, j: (l, 0, j)),
                  pl.BlockSpec((None, 1, tn), lambda l, j: (l, 0, j))],
        out_specs=pl.BlockSpec((None, r, tn), lambda l, j: (l, 0, j)),
        out_shape=jax.ShapeDtypeStruct((depth, r, n), f32),
        compiler_params=_cp("parallel", "parallel"),
        name="ada_mod",
    )(cond, w_ada, b_ada.reshape(depth, 1, n))


def _norm_kernel(*refs, combine, final, packed):
    it = iter(refs)
    x_ref = next(it)
    if combine:
        y0_ref, y1_ref, gate_ref = next(it), next(it), next(it)
    g_ref = next(it)
    if not final:
        shift_ref, scale_ref = next(it), next(it)
    x = x_ref[...]
    if combine:
        x = x + gate_ref[...] * (y0_ref[...] + y1_ref[...])
        if not final:
            next(it)[...] = x
    y = x * lax.rsqrt(jnp.mean(x * x, axis=-1, keepdims=True) + NORM_EPS) * g_ref[...]
    if final:
        next(it)[...] = y
        return
    h = (y * (1.0 + scale_ref[...]) + shift_ref[...]).astype(bf16)
    next(it)[...] = h
    if packed:
        bits = lax.bitcast_convert_type(h.astype(f32), jnp.uint32)
        half = bits.shape[1] // 2
        next(it)[...] = (bits[:, half:] & jnp.uint32(0xFFFF0000)) | (bits[:, :half] >> 16)


def _norm(x, g, mod4, tc, s_lat, *, sel=None, moe=None, final=False, packed=False):
    t, d = x.shape
    tm = NORM_TILE
    nt = t // tm
    combine = moe is not None
    row = lambda i: (i, 0)
    modspec = lambda j: pl.BlockSpec((None, None, 1, d), lambda i: (_cond_row(i * tm, tc, s_lat), j, 0, 0))
    args, specs = [x], [pl.BlockSpec((tm, d), row)]
    if combine:
        y, mod4_moe, gate_j = moe
        args += [y, y, mod4_moe]
        specs += [pl.BlockSpec((tm, d), row), pl.BlockSpec((tm, d), lambda i: (i + nt, 0)), modspec(gate_j)]
    args.append(g.reshape(1, d))
    specs.append(pl.BlockSpec((1, d), lambda i: (0, 0)))
    if not final:
        args += [mod4, mod4]
        specs += [modspec(3 * sel), modspec(3 * sel + 1)]
    shapes, ospecs = [], []
    if combine and not final:
        shapes.append(jax.ShapeDtypeStruct((t, d), f32))
        ospecs.append(pl.BlockSpec((tm, d), row))
    if final:
        shapes.append(jax.ShapeDtypeStruct((t, d), f32))
        ospecs.append(pl.BlockSpec((tm, d), row))
    else:
        shapes.append(jax.ShapeDtypeStruct((t, d), bf16))
        ospecs.append(pl.BlockSpec((tm, d), row))
        if packed:
            shapes.append(jax.ShapeDtypeStruct((t, d // 2), jnp.uint32))
            ospecs.append(pl.BlockSpec((tm, d // 2), row))
    return pl.pallas_call(
        functools.partial(_norm_kernel, combine=combine, final=final, packed=packed),
        grid=(nt,),
        in_specs=specs,
        out_specs=ospecs,
        out_shape=shapes,
        input_output_aliases={0: 0} if (combine and not final) else {},
        compiler_params=_cp("parallel"),
        name="norm",
    )(*args)


def _mm_kernel(*refs, residual):
    if residual:
        a_ref, b_ref, x_ref, gate_ref, o_ref = refs
    else:
        a_ref, b_ref, o_ref = refs
    acc = jnp.dot(a_ref[...], b_ref[...], preferred_element_type=f32)
    if residual:
        acc = x_ref[...] + gate_ref[...] * acc
    o_ref[...] = acc.astype(o_ref.dtype)


def _matmul(a, b, out_dtype, *, residual=None):
    m, k = a.shape
    n = b.shape[1]
    tm, tn = _pick(m, ROW_TILE), _pick(n, COL_TILE)
    args = [a, b]
    specs = [pl.BlockSpec((tm, k), lambda i, j: (i, 0)), pl.BlockSpec((k, tn), lambda i, j: (0, j))]
    aliases = {}
    if residual is not None:
        x, mod4, gate_j, tc, s_lat = residual
        args += [x, mod4]
        specs += [pl.BlockSpec((tm, tn), lambda i, j: (i, j)),
                  pl.BlockSpec((None, None, 1, tn), lambda i, j: (_cond_row(i * tm, tc, s_lat), gate_j, 0, j))]
        aliases = {2: 0}
    return pl.pallas_call(
        functools.partial(_mm_kernel, residual=residual is not None),
        grid=(m // tm, n // tn),
        in_specs=specs,
        out_specs=pl.BlockSpec((tm, tn), lambda i, j: (i, j)),
        out_shape=jax.ShapeDtypeStruct((m, n), out_dtype),
        input_output_aliases=aliases,
        compiler_params=_cp("parallel", "parallel"),
        name="matmul",
    )(*args)


def _down_kernel(a_ref, w_ref, gq_ref, gkv_ref, cq_ref, ckv_ref, kr_ref, *, q_lora, kv_lora):
    acc = jnp.dot(a_ref[...], w_ref[...], preferred_element_type=f32)
    cq = acc[:, :q_lora]
    cq = cq * lax.rsqrt(jnp.mean(cq * cq, axis=-1, keepdims=True) + NORM_EPS) * gq_ref[...]
    cq_ref[...] = cq.astype(bf16)
    ckv = acc[:, q_lora:q_lora + kv_lora]
    ckv_ref[...] = ckv * lax.rsqrt(jnp.mean(ckv * ckv, axis=-1, keepdims=True) + NORM_EPS) * gkv_ref[...]
    kr_ref[...] = acc[:, q_lora + kv_lora:]


def _mla_down(h, w_down_dup, g_q, g_kv):
    t, d = h.shape
    q_lora, kv_lora = g_q.shape[0], g_kv.shape[0]
    n = w_down_dup.shape[1]
    tm = _pick(t, ROW_TILE)
    return pl.pallas_call(
        functools.partial(_down_kernel, q_lora=q_lora, kv_lora=kv_lora),
        grid=(t // tm,),
        in_specs=[pl.BlockSpec((tm, d), lambda i: (i, 0)),
                  pl.BlockSpec((d, n), lambda i: (0, 0)),
                  pl.BlockSpec((1, q_lora), lambda i: (0, 0)),
                  pl.BlockSpec((1, kv_lora), lambda i: (0, 0))],
        out_specs=[pl.BlockSpec((tm, q_lora), lambda i: (i, 0)),
                   pl.BlockSpec((tm, kv_lora), lambda i: (i, 0)),
                   pl.BlockSpec((tm, LANES), lambda i: (i, 0))],
        out_shape=[jax.ShapeDtypeStruct((t, q_lora), bf16),
                   jax.ShapeDtypeStruct((t, kv_lora), f32),
                   jax.ShapeDtypeStruct((t, LANES), f32)],
        compiler_params=_cp("parallel"),
        name="mla_down",
    )(h, w_down_dup, g_q.reshape(1, -1), g_kv.reshape(1, -1))


def _rope_tables(s_lat):
    half = ROPE_DIM // 2
    quarter = half // 2
    tpos = jnp.arange(s_lat)
    inv = ROPE_THETA ** (-jnp.arange(0, half, 2, dtype=f32) / half)
    ang_r = (tpos // GRID_W).astype(f32)[:, None] * inv
    ang_c = (tpos % GRID_W).astype(f32)[:, None] * inv
    ang = jnp.concatenate([ang_r, ang_r, ang_c, ang_c], axis=-1)
    ang = jnp.tile(ang, (1, LANES // ROPE_DIM))
    first = (jnp.arange(LANES) % half) < quarter
    sin = jnp.sin(ang)
    return jnp.cos(ang), jnp.where(first, -sin, 0.0), jnp.where(first, 0.0, sin)


def _rope(x, cos, sin_up, sin_dn):
    quarter = ROPE_DIM // 4
    return x * cos + pltpu.roll(x, LANES - quarter, 1) * sin_up + pltpu.roll(x, quarter, 1) * sin_dn


def _softmax_parts(s):
    m = jnp.max(s, axis=-1, keepdims=True)
    e = jnp.exp(s - m)
    return e, jnp.sum(e, axis=-1, keepdims=True)


def _diff_kernel(*refs, cache, tq, s_self, s_cache, lam_init):
    lam_ref, q_ref, k_ref, v_ref = refs[:4]
    if cache:
        kc_ref, vc_ref, cos_ref, sup_ref, sdn_ref, g_ref, o_ref, kall, vall = refs[4:]
        qi = pl.program_id(2)

        @pl.when(qi == 0)
        def _():
            kall[0:s_self, :] = _rope(k_ref[...], cos_ref[...], sup_ref[...], sdn_ref[...]).astype(bf16)
            kall[s_self:s_self + s_cache, :] = kc_ref[...].astype(bf16)
            vall[0:s_self, :] = v_ref[...].astype(bf16)
            vall[s_self:s_self + s_cache, :] = vc_ref[...].astype(bf16)

        rows = pl.ds(pl.multiple_of(qi * tq, tq), tq)
        q = _rope(q_ref[...], cos_ref[rows, :], sup_ref[rows, :], sdn_ref[rows, :])
        kk, vv = kall[...], vall[...]
    else:
        g_ref, o_ref = refs[4:]
        q = q_ref[...]
        kk, vv = k_ref[...].astype(bf16), v_ref[...].astype(bf16)
    dqk = HEAD_DIM // 2
    q = q * (dqk ** -0.5)
    lane = lax.broadcasted_iota(jnp.int32, q.shape, 1)
    e1, l1 = _softmax_parts(_dot_nt(jnp.where(lane < dqk, q, 0.0).astype(bf16), kk))
    e2, l2 = _softmax_parts(_dot_nt(jnp.where(lane >= dqk, q, 0.0).astype(bf16), kk))
    a = e1 * (1.0 / l1) - e2 * (lam_ref[0] / l2)
    o = jnp.dot(a.astype(bf16), vv, preferred_element_type=f32)
    o = o * lax.rsqrt(jnp.mean(o * o, axis=-1, keepdims=True) + NORM_EPS) * g_ref[...]
    o_ref[...] = (o * (1.0 - lam_init)).astype(bf16)


def _soft_kernel(q_ref, k_ref, v_ref, o_ref):
    s = _dot_nt(q_ref[...].astype(bf16), k_ref[...].astype(bf16)) * (HEAD_DIM ** -0.5)
    e, l = _softmax_parts(s)
    o = jnp.dot(e.astype(bf16), v_ref[...].astype(bf16), preferred_element_type=f32)
    o_ref[...] = (o * (1.0 / l)).astype(bf16)


def _nbr_kernel(ks_ref, cfg_ref, q_ref, k_ref, v_ref, kc_ref, vc_ref, bias_ref, o_ref, *, kw):
    del cfg_ref
    t = pl.program_id(2)
    rows = pl.ds(pl.multiple_of(ks_ref[t] * GRID_W, GRID_W), kw)
    scale = HEAD_DIM ** -0.5
    q = q_ref[...].astype(bf16)
    s_loc = _dot_nt(q, k_ref[rows, :].astype(bf16)) * scale + bias_ref[...]
    s_ctx = _dot_nt(q, kc_ref[...].astype(bf16)) * scale
    m = jnp.maximum(jnp.max(s_loc, axis=-1, keepdims=True), jnp.max(s_ctx, axis=-1, keepdims=True))
    e_loc, e_ctx = jnp.exp(s_loc - m), jnp.exp(s_ctx - m)
    l = jnp.sum(e_loc, axis=-1, keepdims=True) + jnp.sum(e_ctx, axis=-1, keepdims=True)
    o = (jnp.dot(e_loc.astype(bf16), v_ref[rows, :].astype(bf16), preferred_element_type=f32)
         + jnp.dot(e_ctx.astype(bf16), vc_ref[...].astype(bf16), preferred_element_type=f32))
    o_ref[...] = (o * (1.0 / l)).astype(bf16)


def _mla_kernel(*refs, cache, tq, s_self, s_cache):
    qn_ref, qr_ref, kn_ref, v_ref, kr_ref = refs[:5]
    if cache:
        knc_ref, vc_ref, krc_ref, cos_ref, sup_ref, sdn_ref, o_ref, kcat, vall = refs[5:]
    else:
        o_ref, kcat, vall = refs[5:]
    qi = pl.program_id(2)
    hd = NOPE_DIM

    @pl.when(qi == 0)
    def _():
        kr = kr_ref[...]
        if cache:
            kr = _rope(kr, cos_ref[...], sup_ref[...], sdn_ref[...])
        kr = kr.astype(bf16)
        for hh in range(2):
            kcat[hh, 0:s_self, 0:hd] = kn_ref[:, hh * hd:(hh + 1) * hd]
            kcat[hh, 0:s_self, hd:2 * hd] = kr
        vall[0:s_self, :] = v_ref[...]
        if cache:
            krc = krc_ref[...].astype(bf16)
            for hh in range(2):
                kcat[hh, s_self:s_self + s_cache, 0:hd] = knc_ref[:, hh * hd:(hh + 1) * hd]
                kcat[hh, s_self:s_self + s_cache, hd:2 * hd] = krc
            vall[s_self:s_self + s_cache, :] = vc_ref[...]

    qr = qr_ref[...]
    if cache:
        rows = pl.ds(pl.multiple_of(qi * tq, tq), tq)
        qr = _rope(qr, cos_ref[rows, :], sup_ref[rows, :], sdn_ref[rows, :])
    lane = lax.broadcasted_iota(jnp.int32, qr.shape, 1)
    scale = (NOPE_DIM + ROPE_DIM) ** -0.5
    qn = qn_ref[...]
    for hh in range(2):
        mask = (lane < ROPE_DIM) if hh == 0 else (lane >= ROPE_DIM)
        qc = jnp.concatenate([qn[:, hh * hd:(hh + 1) * hd], jnp.where(mask, qr, 0.0)], axis=-1).astype(bf16)
        e, l = _softmax_parts(_dot_nt(qc, kcat[hh]) * scale)
        o = jnp.dot(e.astype(bf16), vall[:, hh * V_DIM:(hh + 1) * V_DIM], preferred_element_type=f32)
        o_ref[:, hh * V_DIM:(hh + 1) * V_DIM] = (o * (1.0 / l)).astype(bf16)


def _nbr_tables(rows):
    r_tile = min(NBR_ROWS, rows)
    kr = min(WIN_ROWS, rows)
    kc = min(WIN_COLS, GRID_W)
    krw = min(rows, r_tile + kr)
    n_tiles = rows // r_tile
    cols = np.arange(GRID_W)
    col_start = np.clip(cols - kc // 2, 0, GRID_W - kc)
    col_ok = (cols[None, :] >= col_start[:, None]) & (cols[None, :] < col_start[:, None] + kc)
    cidx = np.clip(cols[None, :] - cols[:, None], -(WIN_COLS - 1), WIN_COLS - 1) + WIN_COLS - 1
    ks_list, cfg_list, cfgs = [], [], {}
    for t in range(n_tiles):
        qr = t * r_tile + np.arange(r_tile)
        r0 = np.clip(qr - kr // 2, 0, rows - kr)
        ks = int(np.clip(t * r_tile - kr // 2, 0, rows - krw))
        krow = ks + np.arange(krw)
        row_ok = (krow[None, :] >= r0[:, None]) & (krow[None, :] < r0[:, None] + kr)
        ridx = np.clip(krow[None, :] - qr[:, None] + WIN_ROWS - 1, 0, 2 * WIN_ROWS - 2)
        key = (row_ok.tobytes(), ridx.tobytes())
        if key not in cfgs:
            valid = row_ok[:, None, :, None] & col_ok[None, :, None, :]
            shape = valid.shape
            cfgs[key] = (len(cfgs),
                         valid.reshape(r_tile * GRID_W, krw * GRID_W),
                         np.broadcast_to(ridx[:, None, :, None], shape).reshape(r_tile * GRID_W, krw * GRID_W),
                         np.broadcast_to(cidx[None, :, None, :], shape).reshape(r_tile * GRID_W, krw * GRID_W))
        ks_list.append(ks)
        cfg_list.append(cfgs[key][0])
    ordered = sorted(cfgs.values(), key=lambda c: c[0])
    valid = np.stack([c[1] for c in ordered])
    ridx = np.stack([c[2] for c in ordered])
    cidx_all = np.stack([c[3] for c in ordered])
    return (r_tile, krw, np.asarray(ks_list, np.int32), np.asarray(cfg_list, np.int32), valid, ridx, cidx_all)


def _attn_even(p, o_width, lam, lam_init, g_sub, rpb, ck_a, cv_a, ck_b, cv_b, dims, tables):
    bc, seq, bs, s_lat, past = dims
    t = p.shape[0]
    tc = bc * seq
    h_a = ck_a.shape[-1] // HEAD_DIM
    h_b = ck_b.shape[-1] // HEAD_DIM
    cos, sup, sdn = tables
    hd = HEAD_DIM
    lam_arr = jnp.reshape(lam, (1,)).astype(f32)
    g2 = g_sub.reshape(1, hd)
    smem = pl.BlockSpec(memory_space=pltpu.SMEM)
    o_shape = jax.ShapeDtypeStruct((t, o_width), bf16)

    o = pl.pallas_call(
        functools.partial(_diff_kernel, cache=False, tq=seq, s_self=seq, s_cache=0, lam_init=lam_init),
        grid=(bc, h_a, 1),
        in_specs=[smem,
                  pl.BlockSpec((seq, hd), lambda b, h, i: (b, h)),
                  pl.BlockSpec((seq, hd), lambda b, h, i: (b, h_a + h)),
                  pl.BlockSpec((seq, hd), lambda b, h, i: (b, 2 * h_a + h)),
                  pl.BlockSpec((1, hd), lambda b, h, i: (0, 0))],
        out_specs=pl.BlockSpec((seq, hd), lambda b, h, i: (b, h)),
        out_shape=o_shape,
        compiler_params=_cp("parallel", "parallel", "arbitrary"),
        name="attn_diff_ctx",
    )(lam_arr, p, p, p, g2)

    o = pl.pallas_call(
        lambda o_in, q, k, v, o_out: _soft_kernel(q, k, v, o_out),
        grid=(bc, h_b),
        in_specs=[pl.BlockSpec(memory_space=pl.ANY),
                  pl.BlockSpec((seq, hd), lambda b, h: (b, 3 * h_a + h)),
                  pl.BlockSpec((seq, hd), lambda b, h: (b, 3 * h_a + h_b + h)),
                  pl.BlockSpec((seq, hd), lambda b, h: (b, 3 * h_a + 2 * h_b + h))],
        out_specs=pl.BlockSpec((seq, hd), lambda b, h: (b, h_a + h)),
        out_shape=o_shape,
        input_output_aliases={0: 0},
        compiler_params=_cp("parallel", "parallel"),
        name="attn_soft_ctx",
    )(o, p, p, p)

    tq = _pick(s_lat, Q_TILE)
    nq = s_lat // tq
    full = lambda b, h, i: (0, 0)
    o = pl.pallas_call(
        lambda o_in, *refs: _diff_kernel(*refs, cache=True, tq=tq, s_self=s_lat, s_cache=past, lam_init=lam_init),
        grid=(bs, h_a, nq),
        in_specs=[pl.BlockSpec(memory_space=pl.ANY), smem,
                  pl.BlockSpec((tq, hd), lambda b, h, i: (tc // tq + b * nq + i, h)),
                  pl.BlockSpec((s_lat, hd), lambda b, h, i: (tc // s_lat + b, h_a + h)),
                  pl.BlockSpec((s_lat, hd), lambda b, h, i: (tc // s_lat + b, 2 * h_a + h)),
                  pl.BlockSpec((None, past, hd), lambda b, h, i: (b, 0, h)),
                  pl.BlockSpec((None, past, hd), lambda b, h, i: (b, 0, h)),
                  pl.BlockSpec((s_lat, hd), full), pl.BlockSpec((s_lat, hd), full), pl.BlockSpec((s_lat, hd), full),
                  pl.BlockSpec((1, hd), full)],
        out_specs=pl.BlockSpec((tq, hd), lambda b, h, i: (tc // tq + b * nq + i, h)),
        out_shape=o_shape,
        scratch_shapes=[pltpu.VMEM((s_lat + past, hd), bf16), pltpu.VMEM((s_lat + past, hd), bf16)],
        input_output_aliases={0: 0},
        compiler_params=_cp("parallel", "parallel", "arbitrary"),
        name="attn_diff_lat",
    )(o, lam_arr, p, p, p, ck_a, cv_a, cos, sup, sdn, g2)

    rows = s_lat // GRID_W
    r_tile, krw, ks_np, cfg_np, valid, ridx, cidx = _nbr_tables(rows)
    tqn, kw = r_tile * GRID_W, krw * GRID_W
    nqn = rows // r_tile
    bias = jnp.where(valid[None], rpb[:, ridx, cidx].astype(f32), NEG_INF)
    o = pl.pallas_call(
        lambda ks, cfg, o_in, *refs: _nbr_kernel(ks, cfg, *refs, kw=kw),
        grid_spec=pltpu.PrefetchScalarGridSpec(
            num_scalar_prefetch=2,
            grid=(bs, h_b, nqn),
            in_specs=[pl.BlockSpec(memory_space=pl.ANY),
                      pl.BlockSpec((tqn, hd), lambda b, h, i, ks, cfg: (tc // tqn + b * nqn + i, 3 * h_a + h)),
                      pl.BlockSpec((s_lat, hd), lambda b, h, i, ks, cfg: (tc // s_lat + b, 3 * h_a + h_b + h)),
                      pl.BlockSpec((s_lat, hd), lambda b, h, i, ks, cfg: (tc // s_lat + b, 3 * h_a + 2 * h_b + h)),
                      pl.BlockSpec((None, past, hd), lambda b, h, i, ks, cfg: (b, 0, h)),
                      pl.BlockSpec((None, past, hd), lambda b, h, i, ks, cfg: (b, 0, h)),
                      pl.BlockSpec((None, None, tqn, kw), lambda b, h, i, ks, cfg: (h, cfg[i], 0, 0))],
            out_specs=pl.BlockSpec((tqn, hd), lambda b, h, i, ks, cfg: (tc // tqn + b * nqn + i, h_a + h)),
        ),
        out_shape=o_shape,
        input_output_aliases={2: 0},
        compiler_params=_cp("parallel", "parallel", "arbitrary"),
        name="attn_nbr_lat",
    )(jnp.asarray(ks_np), jnp.asarray(cfg_np), o, p, p, p, ck_b, cv_b, bias)
    return o


def _attn_mla(q, kv, krd, krd_cache, h_c, dims, tables):
    bc, seq, bs, s_lat, past = dims
    t = q.shape[0]
    tc = bc * seq
    cos, sup, sdn = tables
    hp = h_c // 2
    w2 = 2 * NOPE_DIM
    o_shape = jax.ShapeDtypeStruct((t, h_c * V_DIM), bf16)

    o = pl.pallas_call(
        functools.partial(_mla_kernel, cache=False, tq=seq, s_self=seq, s_cache=0),
        grid=(bc, hp, 1),
        in_specs=[pl.BlockSpec((seq, w2), lambda b, h, i: (b, h)),
                  pl.BlockSpec((seq, LANES), lambda b, h, i: (b, 2 * hp + h)),
                  pl.BlockSpec((seq, w2), lambda b, h, i: (b, h)),
                  pl.BlockSpec((seq, w2), lambda b, h, i: (b, hp + h)),
                  pl.BlockSpec((seq, LANES), lambda b, h, i: (b, 0))],
        out_specs=pl.BlockSpec((seq, w2), lambda b, h, i: (b, h)),
        out_shape=o_shape,
        scratch_shapes=[pltpu.VMEM((2, seq, w2), bf16), pltpu.VMEM((seq, w2), bf16)],
        compiler_params=_cp("parallel", "parallel", "arbitrary"),
        name="attn_mla_ctx",
    )(q, q, kv, kv, krd)

    tq = _pick(s_lat, Q_TILE)
    nq = s_lat // tq
    full = lambda b, h, i: (0, 0)
    o = pl.pallas_call(
        lambda o_in, *refs: _mla_kernel(*refs, cache=True, tq=tq, s_self=s_lat, s_cache=past),
        grid=(bs, hp, nq),
        in_specs=[pl.BlockSpec(memory_space=pl.ANY),
                  pl.BlockSpec((tq, w2), lambda b, h, i: (tc // tq + b * nq + i, h)),
                  pl.BlockSpec((tq, LANES), lambda b, h, i: (tc // tq + b * nq + i, 2 * hp + h)),
                  pl.BlockSpec((s_lat, w2), lambda b, h, i: (tc // s_lat + b, h)),
                  pl.BlockSpec((s_lat, w2), lambda b, h, i: (tc // s_lat + b, hp + h)),
                  pl.BlockSpec((s_lat, LANES), lambda b, h, i: (tc // s_lat + b, 0)),
                  pl.BlockSpec((past, w2), lambda b, h, i: (t // past + b, h)),
                  pl.BlockSpec((past, w2), lambda b, h, i: (t // past + b, hp + h)),
                  pl.BlockSpec((None, past, LANES), lambda b, h, i: (b, 0, 0)),
                  pl.BlockSpec((s_lat, LANES), full), pl.BlockSpec((s_lat, LANES), full),
                  pl.BlockSpec((s_lat, LANES), full)],
        out_specs=pl.BlockSpec((tq, w2), lambda b, h, i: (tc // tq + b * nq + i, h)),
        out_shape=o_shape,
        scratch_shapes=[pltpu.VMEM((2, s_lat + past, w2), bf16), pltpu.VMEM((s_lat + past, w2), bf16)],
        input_output_aliases={0: 0},
        compiler_params=_cp("parallel", "parallel", "arbitrary"),
        name="attn_mla_lat",
    )(o, q, q, kv, kv, krd, kv, kv, krd_cache, cos, sup, sdn)
    return o


def _router_kernel(h_ref, w_ref, b_ref, eid_ref, wgt_ref):
    logits = jnp.dot(h_ref[...], w_ref[...], preferred_element_type=f32) + b_ref[...]
    lane = lax.broadcasted_iota(jnp.int32, logits.shape, 1)
    big = jnp.int32(LANES)
    neg = -jnp.inf

    def top(vals):
        v = jnp.max(vals, axis=-1, keepdims=True)
        return v, jnp.min(jnp.where(vals == v, lane, big), axis=-1, keepdims=True)

    gl = jnp.where(lane < N_GROUPS, logits, neg)
    gmax, g_idx = top(gl)
    p_group = 1.0 / jnp.sum(jnp.exp(gl - gmax), axis=-1, keepdims=True)
    lo = N_GROUPS + g_idx * EXPERTS_PER_GROUP
    el = jnp.where((lane >= lo) & (lane < lo + EXPERTS_PER_GROUP), logits, neg)
    v1, i1 = top(el)
    v2, i2 = top(jnp.where(lane == i1, neg, el))
    e = jnp.exp(v2 - v1)
    w1 = p_group / (1.0 + e)
    eid_ref[...] = jnp.where(lane == 0, i1 - N_GROUPS, jnp.where(lane == 1, i2 - N_GROUPS, 0))
    wgt_ref[...] = jnp.where(lane == 0, w1, jnp.where(lane == 1, w1 * e, 0.0))


def _router(h, w_r, b_r):
    t, d = h.shape
    tm = _pick(t, ROW_TILE)
    return pl.pallas_call(
        _router_kernel,
        grid=(t // tm,),
        in_specs=[pl.BlockSpec((tm, d), lambda i: (i, 0)),
                  pl.BlockSpec((d, LANES), lambda i: (0, 0)),
                  pl.BlockSpec((1, LANES), lambda i: (0, 0))],
        out_specs=[pl.BlockSpec((tm, LANES), lambda i: (i, 0)), pl.BlockSpec((tm, LANES), lambda i: (i, 0))],
        out_shape=[jax.ShapeDtypeStruct((t, LANES), jnp.int32), jax.ShapeDtypeStruct((t, LANES), f32)],
        compiler_params=_cp("parallel"),
        name="router",
    )(h, w_r, b_r)


def _moe_kernel(te_ref, nu_ref, tok_ref, dst_ref, hp_ref, rw_ref, w1_ref, w3_ref, w2_ref, y_ref,
                xg, yb, gsem, ssem, *, tm):
    del te_ref
    i = pl.program_id(0)

    @pl.when(i < nu_ref[0])
    def _():
        base = i * tm

        def gather(r, carry):
            pltpu.make_async_copy(hp_ref.at[pl.ds(tok_ref[base + r], 1)], xg.at[pl.ds(r, 1)], gsem).start()
            return carry

        lax.fori_loop(0, tm, gather, 0)
        pltpu.make_async_copy(hp_ref.at[pl.ds(0, tm)], xg, gsem).wait()
        u = xg[...]
        half = u.shape[1]
        x_lo = lax.bitcast_convert_type(u << 16, f32).astype(bf16)
        x_hi = lax.bitcast_convert_type(u & jnp.uint32(0xFFFF0000), f32).astype(bf16)
        a = (jnp.dot(x_lo, w1_ref[0:half, :], preferred_element_type=f32)
             + jnp.dot(x_hi, w1_ref[half:2 * half, :], preferred_element_type=f32))
        b = (jnp.dot(x_lo, w3_ref[0:half, :], preferred_element_type=f32)
             + jnp.dot(x_hi, w3_ref[half:2 * half, :], preferred_element_type=f32))
        hid = (jax.nn.silu(a) * b * rw_ref[...]).astype(bf16)
        yb[...] = jnp.dot(hid, w2_ref[...], preferred_element_type=f32)

        def scatter(r, carry):
            pltpu.make_async_copy(yb.at[pl.ds(r, 1)], y_ref.at[pl.ds(dst_ref[base + r], 1)], ssem).start()
            return carry

        lax.fori_loop(0, tm, scatter, 0)
        pltpu.make_async_copy(yb, y_ref.at[pl.ds(0, tm)], ssem).wait()


def _moe(hp, eid, wgt, w1, w3, w2):
    t, half = hp.shape
    d = 2 * half
    n_exp, _, f = w1.shape
    tm = MOE_TILE
    n_assign = 2 * t
    n_tiles = n_assign // tm + n_exp
    n_rows = n_tiles * tm

    e_flat = eid.T.reshape(-1)
    w_flat = wgt.T.reshape(-1)
    order = jnp.argsort(e_flat, stable=True).astype(jnp.int32)
    counts = jnp.sum(e_flat[:, None] == jnp.arange(n_exp)[None, :], axis=0).astype(jnp.int32)
    padded = ((counts + tm - 1) // tm) * tm
    pad_end = jnp.cumsum(padded)
    pad_start = pad_end - padded
    src_start = jnp.cumsum(counts) - counts
    n_used = (pad_end[-1] // tm).astype(jnp.int32).reshape(1)
    tile_e = jnp.minimum(jnp.searchsorted(pad_end, jnp.arange(n_tiles) * tm, side="right"), n_exp - 1).astype(jnp.int32)
    rows = jnp.arange(n_rows, dtype=jnp.int32)
    row_e = jnp.repeat(tile_e, tm)
    j = rows - pad_start[row_e]
    valid = (j < counts[row_e]) & (rows < pad_end[-1])
    a = order[jnp.clip(src_start[row_e] + j, 0, n_assign - 1)]
    row_tok = jnp.where(valid, a % t, 0).astype(jnp.int32)
    row_dst = jnp.where(valid, a, n_assign + rows % tm).astype(jnp.int32)
    row_w = jnp.where(valid, w_flat[a], 0.0).astype(f32).reshape(n_rows, 1)

    return pl.pallas_call(
        functools.partial(_moe_kernel, tm=tm),
        grid_spec=pltpu.PrefetchScalarGridSpec(
            num_scalar_prefetch=4,
            grid=(n_tiles,),
            in_specs=[pl.BlockSpec(memory_space=pl.ANY),
                      pl.BlockSpec((tm, 1), lambda i, te, nu, tok, dst: (i, 0)),
                      pl.BlockSpec((None, d, f), lambda i, te, nu, tok, dst: (te[i], 0, 0)),
                      pl.BlockSpec((None, d, f), lambda i, te, nu, tok, dst: (te[i], 0, 0)),
                      pl.BlockSpec((None, f, d), lambda i, te, nu, tok, dst: (te[i], 0, 0))],
            out_specs=pl.BlockSpec(memory_space=pl.ANY),
            scratch_shapes=[pltpu.VMEM((tm, half), jnp.uint32), pltpu.VMEM((tm, d), f32),
                            pltpu.SemaphoreType.DMA(()), pltpu.SemaphoreType.DMA(())],
        ),
        out_shape=jax.ShapeDtypeStruct((n_assign + tm, d), f32),
        compiler_params=_cp("arbitrary"),
        name="moe",
    )(tile_e, n_used, row_tok, row_dst, hp, row_w, w1, w3, w2)


def kernel(x_prompt, x_sample, c, c_ctx, cache_a_k, cache_a_v, cache_b_k, cache_b_v, cache_c_kv, cache_c_kr,
           g_norm1, g_norm2, g_final, w_ada, b_ada, w_in_ab, w_out_ab, lam_q1, lam_k1, lam_q2, lam_k2,
           g_sub_a, rpb_b, w_down_c, g_q_c, g_kv_c, w_uq_c, w_uk_c, w_uv_c, w_out_c,
           w_group_router, b_group_router, w_expert_router, b_expert_router, w1_moe, w3_moe, w2_moe):
    bc, seq, d = x_prompt.shape
    bs, s_lat, _ = x_sample.shape
    past = cache_a_k.shape[2]
    depth = w_ada.shape[0]
    tc, ts = bc * seq, bs * s_lat
    t = tc + ts
    h_a, h_b = cache_a_k.shape[3], cache_b_k.shape[3]
    ab = (h_a + h_b) * HEAD_DIM
    dims = (bc, seq, bs, s_lat, past)
    assert tc % s_lat == 0 and t % past == 0 and s_lat % NORM_TILE == 0 and tc % ROW_TILE == 0

    x = jnp.concatenate([x_prompt.reshape(tc, d), x_sample.reshape(ts, d)], axis=0)
    cond = jnp.concatenate([c_ctx[None], c, jnp.zeros((COND_ROWS_PAD - 1 - bs, d), f32)], axis=0)
    mod = _ada_mod(cond, w_ada, b_ada)
    tables = _rope_tables(s_lat)

    n_exp = N_GROUPS * EXPERTS_PER_GROUP
    sak, sav, sbk, sbv, sckv, sckr = [], [], [], [], [], []
    moe_pending = None
    for l in range(depth):
        mod4 = mod[l].reshape(COND_ROWS_PAD, 6, 1, d)
        if moe_pending is None:
            h = _norm(x, g_norm1[l], mod4, tc, s_lat, sel=0)[0]
        else:
            x, h = _norm(x, g_norm1[l], mod4, tc, s_lat, sel=0, moe=moe_pending)
        if l % 2 == 0:
            e = l // 2
            lam_init = 0.8 - 0.6 * math.exp(-0.3 * l)
            lam = (jnp.exp(jnp.sum((lam_q1[e] * lam_k1[e]).astype(f32)))
                   - jnp.exp(jnp.sum((lam_q2[e] * lam_k2[e]).astype(f32))) + lam_init)
            p = _matmul(h, w_in_ab[e].astype(bf16), f32)
            wa = h_a * HEAD_DIM
            sak.append(p[:tc, wa:2 * wa].reshape(bc, seq, h_a, HEAD_DIM))
            sav.append(p[:tc, 2 * wa:3 * wa].reshape(bc, seq, h_a, HEAD_DIM))
            wb = h_b * HEAD_DIM
            sbk.append(p[:tc, 3 * wa + wb:3 * wa + 2 * wb].reshape(bc, seq, h_b, HEAD_DIM))
            sbv.append(p[:tc, 3 * wa + 2 * wb:].reshape(bc, seq, h_b, HEAD_DIM))
            o = _attn_even(p, ab, lam, lam_init, g_sub_a[e], rpb_b[e],
                           cache_a_k[:, e].reshape(bs, past, wa), cache_a_v[:, e].reshape(bs, past, wa),
                           cache_b_k[:, e].reshape(bs, past, wb), cache_b_v[:, e].reshape(bs, past, wb),
                           dims, tables)
            w_out = w_out_ab[e].astype(bf16)
        else:
            oi = l // 2
            q_lora, kv_lora = g_q_c.shape[1], g_kv_c.shape[1]
            h_c = w_uk_c.shape[2] // NOPE_DIM
            wd = w_down_c[oi]
            wd = jnp.concatenate([wd, wd[:, q_lora + kv_lora:]], axis=1).astype(bf16)
            cq, ckv, krd = _mla_down(h, wd, g_q_c[oi], g_kv_c[oi])
            sckv.append(ckv[:tc].reshape(bc, seq, kv_lora))
            sckr.append(krd[:tc, :ROPE_DIM].reshape(bc, seq, ROPE_DIM))
            wq = w_uq_c[oi].reshape(q_lora, h_c, NOPE_DIM + ROPE_DIM)
            wq = jnp.concatenate([wq[:, :, :NOPE_DIM].reshape(q_lora, -1), wq[:, :, NOPE_DIM:].reshape(q_lora, -1)],
                                 axis=1).astype(bf16)
            q = _matmul(cq, wq, f32)
            ckv_all = jnp.concatenate([ckv, cache_c_kv[:, oi].reshape(bs * past, kv_lora)], axis=0).astype(bf16)
            wkv = jnp.concatenate([w_uk_c[oi], w_uv_c[oi]], axis=1).astype(bf16)
            kv = _matmul(ckv_all, wkv, bf16)
            krc = cache_c_kr[:, oi]
            o = _attn_mla(q, kv, krd, jnp.concatenate([krc, krc], axis=-1), h_c, dims, tables)
            w_out = w_out_c[oi].astype(bf16)
        x = _matmul(o, w_out, f32, residual=(x, mod4, 2, tc, s_lat))
        h2, hp = _norm(x, g_norm2[l], mod4, tc, s_lat, sel=1, packed=True)
        w_r = jnp.concatenate([w_group_router[l], w_expert_router[l]], axis=1)
        w_r = jnp.pad(w_r, ((0, 0), (0, LANES - w_r.shape[1]))).astype(bf16)
        b_r = jnp.pad(jnp.concatenate([b_group_router[l], b_expert_router[l]]), (0, LANES - N_GROUPS - n_exp))
        eid, wgt = _router(h2, w_r, b_r.reshape(1, LANES).astype(f32))
        f = w1_moe.shape[-1]
        y = _moe(hp, eid[:, :2], wgt[:, :2],
                 w1_moe[l].reshape(n_exp, d, f).astype(bf16), w3_moe[l].reshape(n_exp, d, f).astype(bf16),
                 w2_moe[l].reshape(n_exp, f, d).astype(bf16))
        moe_pending = (y, mod4, 5)
    out = _norm(x, g_final, None, tc, s_lat, moe=moe_pending, final=True)[0]
    y_prompt = out[:tc].reshape(bc, seq, d)
    y_sample = out[tc:].reshape(bs, s_lat, d)
    return (y_prompt, y_sample, jnp.stack(sak, axis=1), jnp.stack(sav, axis=1), jnp.stack(sbk, axis=1),
            jnp.stack(sbv, axis=1), jnp.stack(sckv, axis=1), jnp.stack(sckr, axis=1))
```

```python
import functools
import math

import numpy as np
import jax
import jax.numpy as jnp
from jax import lax
from jax.experimental import pallas as pl
from jax.experimental.pallas import tpu as pltpu

GRID_W = 64
HEAD_DIM = 128
WIN_ROWS = 8
WIN_COLS = 16
NOPE_DIM = 128
ROPE_DIM = 64
V_DIM = 128
N_GROUPS = 4
EXPERTS_PER_GROUP = 8
ROPE_THETA = 10000.0
NORM_EPS = 1e-6
NEG_INF = -1e30
LOG2E = math.log2(math.e)

LANES = 128
COND_ROWS_PAD = 16
VMEM_LIMIT = 56 * 1024 * 1024
ROW_TILE = 512
COL_TILE = 512
NORM_TILE = 256
MOE_TILE = 256
Q_TILE = 256
NBR_ROWS = 4
CTX_HEADS = 8

f32 = jnp.float32
bf16 = jnp.bfloat16


def _cp(*sem):
    return pltpu.CompilerParams(dimension_semantics=sem, vmem_limit_bytes=VMEM_LIMIT)


def _pick(n, pref):
    if n <= pref:
        return n
    t = pref
    while n % t:
        t //= 2
    return t


def _cond_row(r0, tc, s_lat):
    return jnp.where(r0 < tc, 0, 1 + (r0 - tc) // s_lat)


def _dot_nt(a, b):
    return lax.dot_general(a, b, (((1,), (1,)), ((), ())), preferred_element_type=f32)


def _ada_kernel(c_ref, w_ref, b_ref, o_ref):
    s = jax.nn.silu(c_ref[...]).astype(bf16)
    o_ref[...] = jnp.dot(s, w_ref[...].astype(bf16), preferred_element_type=f32) + b_ref[...]


def _ada_mod(cond, w_ada, b_ada):
    depth, d, n = w_ada.shape
    r = cond.shape[0]
    tn = _pick(n, COL_TILE)
    return pl.pallas_call(
        _ada_kernel,
        grid=(depth, n // tn),
        in_specs=[pl.BlockSpec((r, d), lambda l, j: (0, 0)),
                  pl.BlockSpec((None, d, tn), lambda l, j: (l, 0, j)),
                  pl.BlockSpec((None, 1, tn), lambda l, j: (l, 0, j))],
        out_specs=pl.BlockSpec((None, r, tn), lambda l, j: (l, 0, j)),
        out_shape=jax.ShapeDtypeStruct((depth, r, n), f32),
        compiler_params=_cp("parallel", "parallel"),
        name="ada_mod",
    )(cond, w_ada, b_ada.reshape(depth, 1, n))


def _norm_kernel(*refs, combine, final, packed):
    it = iter(refs)
    x_ref = next(it)
    if combine:
        y0_ref, y1_ref, gate_ref = next(it), next(it), next(it)
    g_ref = next(it)
    if not final:
        shift_ref, scale_ref = next(it), next(it)
    x = x_ref[...]
    if combine:
        x = x + gate_ref[...] * (y0_ref[...] + y1_ref[...])
        if not final:
            next(it)[...] = x
    y = x * lax.rsqrt(jnp.mean(x * x, axis=-1, keepdims=True) + NORM_EPS) * g_ref[...]
    if final:
        next(it)[...] = y
        return
    h = (y * (1.0 + scale_ref[...]) + shift_ref[...]).astype(bf16)
    next(it)[...] = h
    if packed:
        bits = lax.bitcast_convert_type(h.astype(f32), jnp.uint32)
        half = bits.shape[1] // 2
        next(it)[...] = (bits[:, half:] & jnp.uint32(0xFFFF0000)) | (bits[:, :half] >> 16)


def _norm(x, g, mod4, tc, s_lat, *, sel=None, moe=None, final=False, packed=False):
    t, d = x.shape
    tm = NORM_TILE
    nt = t // tm
    combine = moe is not None
    row = lambda i: (i, 0)
    modspec = lambda j: pl.BlockSpec((None, None, 1, d), lambda i: (_cond_row(i * tm, tc, s_lat), j, 0, 0))
    args, specs = [x], [pl.BlockSpec((tm, d), row)]
    if combine:
        y, mod4_moe, gate_j = moe
        args += [y, y, mod4_moe]
        specs += [pl.BlockSpec((tm, d), row), pl.BlockSpec((tm, d), lambda i: (i + nt, 0)), modspec(gate_j)]
    args.append(g.reshape(1, d))
    specs.append(pl.BlockSpec((1, d), lambda i: (0, 0)))
    if not final:
        args += [mod4, mod4]
        specs += [modspec(3 * sel), modspec(3 * sel + 1)]
    shapes, ospecs = [], []
    if combine and not final:
        shapes.append(jax.ShapeDtypeStruct((t, d), f32))
        ospecs.append(pl.BlockSpec((tm, d), row))
    if final:
        shapes.append(jax.ShapeDtypeStruct((t, d), f32))
        ospecs.append(pl.BlockSpec((tm, d), row))
    else:
        shapes.append(jax.ShapeDtypeStruct((t, d), bf16))
        ospecs.append(pl.BlockSpec((tm, d), row))
        if packed:
            shapes.append(jax.ShapeDtypeStruct((t, d // 2), jnp.uint32))
            ospecs.append(pl.BlockSpec((tm, d // 2), row))
    return pl.pallas_call(
        functools.partial(_norm_kernel, combine=combine, final=final, packed=packed),
        grid=(nt,),
        in_specs=specs,
        out_specs=ospecs,
        out_shape=shapes,
        input_output_aliases={0: 0} if (combine and not final) else {},
        compiler_params=_cp("parallel"),
        name="norm",
    )(*args)


def _mm_kernel(*refs, residual):
    if residual:
        a_ref, b_ref, x_ref, gate_ref, o_ref = refs
    else:
        a_ref, b_ref, o_ref = refs
    acc = jnp.dot(a_ref[...], b_ref[...], preferred_element_type=f32)
    if residual:
        acc = x_ref[...] + gate_ref[...] * acc
    o_ref[...] = acc.astype(o_ref.dtype)


def _matmul(a, b, out_dtype, *, residual=None):
    m, k = a.shape
    n = b.shape[1]
    tm = _pick(m, 2 * ROW_TILE if (k >= 4 * ROW_TILE or k <= ROW_TILE) else ROW_TILE)
    tn = _pick(n, COL_TILE * max(1, min(4, 4 * ROW_TILE // k)))
    args = [a, b]
    specs = [pl.BlockSpec((tm, k), lambda i, j: (i, 0)), pl.BlockSpec((k, tn), lambda i, j: (0, j))]
    aliases = {}
    if residual is not None:
        x, mod4, gate_j, tc, s_lat = residual
        args += [x, mod4]
        specs += [pl.BlockSpec((tm, tn), lambda i, j: (i, j)),
                  pl.BlockSpec((None, None, 1, tn), lambda i, j: (_cond_row(i * tm, tc, s_lat), gate_j, 0, j))]
        aliases = {2: 0}
    return pl.pallas_call(
        functools.partial(_mm_kernel, residual=residual is not None),
        grid=(m // tm, n // tn),
        in_specs=specs,
        out_specs=pl.BlockSpec((tm, tn), lambda i, j: (i, j)),
        out_shape=jax.ShapeDtypeStruct((m, n), out_dtype),
        input_output_aliases=aliases,
        compiler_params=_cp("parallel", "parallel"),
        name="matmul",
    )(*args)


def _down_kernel(a_ref, w_ref, gq_ref, gkv_ref, cq_ref, ckv_ref, kr_ref, *, q_lora, kv_lora):
    acc = jnp.dot(a_ref[...], w_ref[...], preferred_element_type=f32)
    cq = acc[:, :q_lora]
    cq = cq * lax.rsqrt(jnp.mean(cq * cq, axis=-1, keepdims=True) + NORM_EPS) * gq_ref[...]
    cq_ref[...] = cq.astype(bf16)
    ckv = acc[:, q_lora:q_lora + kv_lora]
    ckv_ref[...] = ckv * lax.rsqrt(jnp.mean(ckv * ckv, axis=-1, keepdims=True) + NORM_EPS) * gkv_ref[...]
    kr_ref[...] = acc[:, q_lora + kv_lora:]


def _mla_down(h, w_down_dup, g_q, g_kv):
    t, d = h.shape
    q_lora, kv_lora = g_q.shape[0], g_kv.shape[0]
    n = w_down_dup.shape[1]
    tm = _pick(t, ROW_TILE)
    return pl.pallas_call(
        functools.partial(_down_kernel, q_lora=q_lora, kv_lora=kv_lora),
        grid=(t // tm,),
        in_specs=[pl.BlockSpec((tm, d), lambda i: (i, 0)),
                  pl.BlockSpec((d, n), lambda i: (0, 0)),
                  pl.BlockSpec((1, q_lora), lambda i: (0, 0)),
                  pl.BlockSpec((1, kv_lora), lambda i: (0, 0))],
        out_specs=[pl.BlockSpec((tm, q_lora), lambda i: (i, 0)),
                   pl.BlockSpec((tm, kv_lora), lambda i: (i, 0)),
                   pl.BlockSpec((tm, LANES), lambda i: (i, 0))],
        out_shape=[jax.ShapeDtypeStruct((t, q_lora), bf16),
                   jax.ShapeDtypeStruct((t, kv_lora), f32),
                   jax.ShapeDtypeStruct((t, LANES), f32)],
        compiler_params=_cp("parallel"),
        name="mla_down",
    )(h, w_down_dup, g_q.reshape(1, -1), g_kv.reshape(1, -1))


def _rope_tables(s_lat):
    half = ROPE_DIM // 2
    quarter = half // 2
    tpos = jnp.arange(s_lat)
    inv = ROPE_THETA ** (-jnp.arange(0, half, 2, dtype=f32) / half)
    ang_r = (tpos // GRID_W).astype(f32)[:, None] * inv
    ang_c = (tpos % GRID_W).astype(f32)[:, None] * inv
    ang = jnp.concatenate([ang_r, ang_r, ang_c, ang_c], axis=-1)
    ang = jnp.tile(ang, (1, LANES // ROPE_DIM))
    first = (jnp.arange(LANES) % half) < quarter
    sin = jnp.sin(ang)
    return jnp.cos(ang), jnp.where(first, -sin, 0.0), jnp.where(first, 0.0, sin)


def _rope(x, cos, sin_up, sin_dn):
    quarter = ROPE_DIM // 4
    return x * cos + pltpu.roll(x, LANES - quarter, 1) * sin_up + pltpu.roll(x, quarter, 1) * sin_dn


def _softmax_parts(s):
    m = jnp.max(s, axis=-1, keepdims=True)
    e = jnp.exp2(s - m)
    return e, jnp.sum(e, axis=-1, keepdims=True)


def _diff_core(q, kk, vv, lam, g, lam_init):
    dqk = HEAD_DIM // 2
    q = q * (dqk ** -0.5 * LOG2E)
    lane = lax.broadcasted_iota(jnp.int32, q.shape, 1)
    e1, l1 = _softmax_parts(_dot_nt(jnp.where(lane < dqk, q, 0.0).astype(bf16), kk))
    e2, l2 = _softmax_parts(_dot_nt(jnp.where(lane >= dqk, q, 0.0).astype(bf16), kk))
    a = e1 * (1.0 / l1) - e2 * (lam / l2)
    o = jnp.dot(a.astype(bf16), vv, preferred_element_type=f32)
    o = o * lax.rsqrt(jnp.mean(o * o, axis=-1, keepdims=True) + NORM_EPS) * g
    return (o * (1.0 - lam_init)).astype(bf16)


def _diff_ctx_kernel(lam_ref, q_ref, k_ref, v_ref, g_ref, o_ref, *, nh, lam_init):
    for hh in range(nh):
        cols = slice(hh * HEAD_DIM, (hh + 1) * HEAD_DIM)
        o_ref[:, cols] = _diff_core(q_ref[:, cols], k_ref[:, cols].astype(bf16), v_ref[:, cols].astype(bf16),
                                    lam_ref[0], g_ref[...], lam_init)


def _diff_lat_kernel(lam_ref, q_ref, k_ref, v_ref, kc_ref, vc_ref, cos_ref, sup_ref, sdn_ref, g_ref, o_ref,
                     kall, vall, *, tq, s_self, s_cache, lam_init):
    qi = pl.program_id(2)

    @pl.when(qi == 0)
    def _():
        kall[0:s_self, :] = _rope(k_ref[...], cos_ref[...], sup_ref[...], sdn_ref[...]).astype(bf16)
        kall[s_self:s_self + s_cache, :] = kc_ref[...].astype(bf16)
        vall[0:s_self, :] = v_ref[...].astype(bf16)
        vall[s_self:s_self + s_cache, :] = vc_ref[...].astype(bf16)

    rows = pl.ds(pl.multiple_of(qi * tq, tq), tq)
    q = _rope(q_ref[...], cos_ref[rows, :], sup_ref[rows, :], sdn_ref[rows, :])
    o_ref[...] = _diff_core(q, kall[...], vall[...], lam_ref[0], g_ref[...], lam_init)


def _soft_ctx_kernel(q_ref, k_ref, v_ref, o_ref, *, nh):
    for hh in range(nh):
        cols = slice(hh * HEAD_DIM, (hh + 1) * HEAD_DIM)
        q = (q_ref[:, cols] * (HEAD_DIM ** -0.5 * LOG2E)).astype(bf16)
        e, l = _softmax_parts(_dot_nt(q, k_ref[:, cols].astype(bf16)))
        o = jnp.dot(e.astype(bf16), v_ref[:, cols].astype(bf16), preferred_element_type=f32)
        o_ref[:, cols] = (o * (1.0 / l)).astype(bf16)


def _nbr_kernel(ks_ref, cfg_ref, q_ref, k_ref, v_ref, kc_ref, vc_ref, bias_ref, o_ref, *, kw):
    del cfg_ref
    t = pl.program_id(2)
    rows = pl.ds(pl.multiple_of(ks_ref[t] * GRID_W, GRID_W), kw)
    q = (q_ref[...] * (HEAD_DIM ** -0.5 * LOG2E)).astype(bf16)
    s_loc = _dot_nt(q, k_ref[rows, :].astype(bf16)) + bias_ref[...]
    s_ctx = _dot_nt(q, kc_ref[...].astype(bf16))
    m = jnp.maximum(jnp.max(s_loc, axis=-1, keepdims=True), jnp.max(s_ctx, axis=-1, keepdims=True))
    e_loc, e_ctx = jnp.exp2(s_loc - m), jnp.exp2(s_ctx - m)
    l = jnp.sum(e_loc, axis=-1, keepdims=True) + jnp.sum(e_ctx, axis=-1, keepdims=True)
    o = (jnp.dot(e_loc.astype(bf16), v_ref[rows, :].astype(bf16), preferred_element_type=f32)
         + jnp.dot(e_ctx.astype(bf16), vc_ref[...].astype(bf16), preferred_element_type=f32))
    o_ref[...] = (o * (1.0 / l)).astype(bf16)


def _mla_head(qn, qr_masked, kcat, vv):
    qc = (jnp.concatenate([qn, qr_masked], axis=-1) * ((NOPE_DIM + ROPE_DIM) ** -0.5 * LOG2E)).astype(bf16)
    e, l = _softmax_parts(_dot_nt(qc, kcat))
    o = jnp.dot(e.astype(bf16), vv, preferred_element_type=f32)
    return (o * (1.0 / l)).astype(bf16)


def _mla_ctx_kernel(qn_ref, qr_ref, kn_ref, v_ref, kr_ref, o_ref, *, npair):
    hd = NOPE_DIM
    kr = kr_ref[...].astype(bf16)
    lane = lax.broadcasted_iota(jnp.int32, (qr_ref.shape[0], LANES), 1)
    for pp in range(npair):
        qr = qr_ref[:, pp * LANES:(pp + 1) * LANES]
        for hh in range(2):
            c = slice((2 * pp + hh) * hd, (2 * pp + hh + 1) * hd)
            mask = (lane < ROPE_DIM) if hh == 0 else (lane >= ROPE_DIM)
            kcat = jnp.concatenate([kn_ref[:, c], kr], axis=-1)
            o_ref[:, c] = _mla_head(qn_ref[:, c], jnp.where(mask, qr, 0.0), kcat, v_ref[:, c])


def _mla_lat_kernel(qn_ref, qr_ref, kn_ref, v_ref, kr_ref, knc_ref, vc_ref, krc_ref, cos_ref, sup_ref, sdn_ref,
                    o_ref, kcat, vall, *, tq, s_self, s_cache):
    qi = pl.program_id(2)
    hd = NOPE_DIM

    @pl.when(qi == 0)
    def _():
        kr = _rope(kr_ref[...], cos_ref[...], sup_ref[...], sdn_ref[...]).astype(bf16)
        krc = krc_ref[...].astype(bf16)
        for hh in range(2):
            kcat[hh, 0:s_self, 0:hd] = kn_ref[:, hh * hd:(hh + 1) * hd]
            kcat[hh, 0:s_self, hd:2 * hd] = kr
            kcat[hh, s_self:s_self + s_cache, 0:hd] = knc_ref[:, hh * hd:(hh + 1) * hd]
            kcat[hh, s_self:s_self + s_cache, hd:2 * hd] = krc
        vall[0:s_self, :] = v_ref[...]
        vall[s_self:s_self + s_cache, :] = vc_ref[...]

    rows = pl.ds(pl.multiple_of(qi * tq, tq), tq)
    qr = _rope(qr_ref[...], cos_ref[rows, :], sup_ref[rows, :], sdn_ref[rows, :])
    lane = lax.broadcasted_iota(jnp.int32, qr.shape, 1)
    for hh in range(2):
        c = slice(hh * hd, (hh + 1) * hd)
        mask = (lane < ROPE_DIM) if hh == 0 else (lane >= ROPE_DIM)
        o_ref[:, c] = _mla_head(qn_ref[:, c], jnp.where(mask, qr, 0.0), kcat[hh], vall[:, c])


def _nbr_tables(rows):
    r_tile = min(NBR_ROWS, rows)
    kr = min(WIN_ROWS, rows)
    krw = min(rows, r_tile + kr)
    n_rel = 2 * WIN_ROWS - 1
    ks_list, cfg_list, cfgs = [], [], {}
    for t in range(rows // r_tile):
        qr = t * r_tile + np.arange(r_tile)
        r0 = np.clip(qr - kr // 2, 0, rows - kr)
        ks = int(np.clip(t * r_tile - kr // 2, 0, rows - krw))
        krow = ks + np.arange(krw)
        row_ok = (krow[None, :] >= r0[:, None]) & (krow[None, :] < r0[:, None] + kr)
        rel = np.where(row_ok, krow[None, :] - qr[:, None] + WIN_ROWS - 1, n_rel).astype(np.int32)
        key = rel.tobytes()
        if key not in cfgs:
            cfgs[key] = (len(cfgs), rel)
        ks_list.append(ks)
        cfg_list.append(cfgs[key][0])
    rel_all = np.stack([c[1] for c in sorted(cfgs.values(), key=lambda c: c[0])])
    return r_tile, krw, np.asarray(ks_list, np.int32), np.asarray(cfg_list, np.int32), rel_all


def _nbr_bias(rpb, rel_all):
    w = GRID_W
    kc = min(WIN_COLS, w)
    h, n_rel, _ = rpb.shape
    span = 2 * w - 1
    off = np.clip(np.arange(span) - (w - 1), -(WIN_COLS - 1), WIN_COLS - 1) + WIN_COLS - 1
    v = rpb[:, :, off].astype(f32) * LOG2E
    m = jnp.tile(v, (1, 1, w + 1))[:, :, :w * (span + 1)].reshape(h, n_rel, w, span + 1)[:, :, :, :w]
    toep = m[:, :, ::-1, :]
    cols = np.arange(w)
    col_start = np.clip(cols - kc // 2, 0, w - kc)
    col_ok = (cols[None, :] >= col_start[:, None]) & (cols[None, :] < col_start[:, None] + kc)
    toep = jnp.where(col_ok, toep, NEG_INF)
    toep = jnp.concatenate([toep, jnp.full((h, 1, w, w), NEG_INF, f32)], axis=1)
    n_cfg, r_tile, krw = rel_all.shape
    b = toep[:, rel_all.reshape(-1)].reshape(h, n_cfg, r_tile, krw, w, w)
    return b.transpose(0, 1, 2, 4, 3, 5).reshape(h, n_cfg, r_tile * w, krw * w)


def _attn_even(p, o_width, lam, lam_init, g_sub, rpb, ck_a, cv_a, ck_b, cv_b, dims, tables):
    bc, seq, bs, s_lat, past = dims
    t = p.shape[0]
    tc = bc * seq
    h_a = ck_a.shape[-1] // HEAD_DIM
    h_b = ck_b.shape[-1] // HEAD_DIM
    cos, sup, sdn = tables
    hd = HEAD_DIM
    lam_arr = jnp.reshape(lam, (1,)).astype(f32)
    g2 = g_sub.reshape(1, hd)
    smem = pl.BlockSpec(memory_space=pltpu.SMEM)
    o_shape = jax.ShapeDtypeStruct((t, o_width), bf16)

    nh = _pick(h_a, CTX_HEADS)
    ga = h_a // nh
    wb = nh * hd
    o = pl.pallas_call(
        functools.partial(_diff_ctx_kernel, nh=nh, lam_init=lam_init),
        grid=(bc, ga),
        in_specs=[smem,
                  pl.BlockSpec((seq, wb), lambda b, h: (b, h)),
                  pl.BlockSpec((seq, wb), lambda b, h: (b, ga + h)),
                  pl.BlockSpec((seq, wb), lambda b, h: (b, 2 * ga + h)),
                  pl.BlockSpec((1, hd), lambda b, h: (0, 0))],
        out_specs=pl.BlockSpec((seq, wb), lambda b, h: (b, h)),
        out_shape=o_shape,
        compiler_params=_cp("parallel", "parallel"),
        name="attn_diff_ctx",
    )(lam_arr, p, p, p, g2)

    nhb = _pick(h_b, CTX_HEADS)
    gb = h_b // nhb
    wbb = nhb * hd
    base = 3 * h_a * hd // wbb
    o = pl.pallas_call(
        lambda o_in, q, k, v, o_out: _soft_ctx_kernel(q, k, v, o_out, nh=nhb),
        grid=(bc, gb),
        in_specs=[pl.BlockSpec(memory_space=pl.ANY),
                  pl.BlockSpec((seq, wbb), lambda b, h: (b, base + h)),
                  pl.BlockSpec((seq, wbb), lambda b, h: (b, base + gb + h)),
                  pl.BlockSpec((seq, wbb), lambda b, h: (b, base + 2 * gb + h))],
        out_specs=pl.BlockSpec((seq, wbb), lambda b, h: (b, h_a * hd // wbb + h)),
        out_shape=o_shape,
        input_output_aliases={0: 0},
        compiler_params=_cp("parallel", "parallel"),
        name="attn_soft_ctx",
    )(o, p, p, p)

    tq = _pick(s_lat, Q_TILE)
    nq = s_lat // tq
    full = lambda b, h, i: (0, 0)
    o = pl.pallas_call(
        lambda o_in, *refs: _diff_lat_kernel(*refs, tq=tq, s_self=s_lat, s_cache=past, lam_init=lam_init),
        grid=(bs, h_a, nq),
        in_specs=[pl.BlockSpec(memory_space=pl.ANY), smem,
                  pl.BlockSpec((tq, hd), lambda b, h, i: (tc // tq + b * nq + i, h)),
                  pl.BlockSpec((s_lat, hd), lambda b, h, i: (tc // s_lat + b, h_a + h)),
                  pl.BlockSpec((s_lat, hd), lambda b, h, i: (tc // s_lat + b, 2 * h_a + h)),
                  pl.BlockSpec((None, past, hd), lambda b, h, i: (b, 0, h)),
                  pl.BlockSpec((None, past, hd), lambda b, h, i: (b, 0, h)),
                  pl.BlockSpec((s_lat, hd), full), pl.BlockSpec((s_lat, hd), full), pl.BlockSpec((s_lat, hd), full),
                  pl.BlockSpec((1, hd), full)],
        out_specs=pl.BlockSpec((tq, hd), lambda b, h, i: (tc // tq + b * nq + i, h)),
        out_shape=o_shape,
        scratch_shapes=[pltpu.VMEM((s_lat + past, hd), bf16), pltpu.VMEM((s_lat + past, hd), bf16)],
        input_output_aliases={0: 0},
        compiler_params=_cp("parallel", "parallel", "arbitrary"),
        name="attn_diff_lat",
    )(o, lam_arr, p, p, p, ck_a, cv_a, cos, sup, sdn, g2)

    rows = s_lat // GRID_W
    r_tile, krw, ks_np, cfg_np, rel_all = _nbr_tables(rows)
    tqn, kw = r_tile * GRID_W, krw * GRID_W
    nqn = rows // r_tile
    bias = _nbr_bias(rpb, rel_all)
    o = pl.pallas_call(
        lambda ks, cfg, o_in, *refs: _nbr_kernel(ks, cfg, *refs, kw=kw),
        grid_spec=pltpu.PrefetchScalarGridSpec(
            num_scalar_prefetch=2,
            grid=(bs, h_b, nqn),
            in_specs=[pl.BlockSpec(memory_space=pl.ANY),
                      pl.BlockSpec((tqn, hd), lambda b, h, i, ks, cfg: (tc // tqn + b * nqn + i, 3 * h_a + h)),
                      pl.BlockSpec((s_lat, hd), lambda b, h, i, ks, cfg: (tc // s_lat + b, 3 * h_a + h_b + h)),
                      pl.BlockSpec((s_lat, hd), lambda b, h, i, ks, cfg: (tc // s_lat + b, 3 * h_a + 2 * h_b + h)),
                      pl.BlockSpec((None, past, hd), lambda b, h, i, ks, cfg: (b, 0, h)),
                      pl.BlockSpec((None, past, hd), lambda b, h, i, ks, cfg: (b, 0, h)),
                      pl.BlockSpec((None, None, tqn, kw), lambda b, h, i, ks, cfg: (h, cfg[i], 0, 0))],
            out_specs=pl.BlockSpec((tqn, hd), lambda b, h, i, ks, cfg: (tc // tqn + b * nqn + i, h_a + h)),
        ),
        out_shape=o_shape,
        input_output_aliases={2: 0},
        compiler_params=_cp("parallel", "parallel", "arbitrary"),
        name="attn_nbr_lat",
    )(jnp.asarray(ks_np), jnp.asarray(cfg_np), o, p, p, p, ck_b, cv_b, bias)
    return o


def _attn_mla(q, kv, krd, krd_cache, h_c, dims, tables):
    bc, seq, bs, s_lat, past = dims
    t = q.shape[0]
    tc = bc * seq
    cos, sup, sdn = tables
    hp = h_c // 2
    w2 = 2 * NOPE_DIM
    o_shape = jax.ShapeDtypeStruct((t, h_c * V_DIM), bf16)

    npair = _pick(hp, CTX_HEADS // 2)
    gp = hp // npair
    wn, wr = npair * w2, npair * LANES
    o = pl.pallas_call(
        functools.partial(_mla_ctx_kernel, npair=npair),
        grid=(bc, gp),
        in_specs=[pl.BlockSpec((seq, wn), lambda b, h: (b, h)),
                  pl.BlockSpec((seq, wr), lambda b, h: (b, 2 * gp + h)),
                  pl.BlockSpec((seq, wn), lambda b, h: (b, h)),
                  pl.BlockSpec((seq, wn), lambda b, h: (b, gp + h)),
                  pl.BlockSpec((seq, LANES), lambda b, h: (b, 0))],
        out_specs=pl.BlockSpec((seq, wn), lambda b, h: (b, h)),
        out_shape=o_shape,
        compiler_params=_cp("parallel", "parallel"),
        name="attn_mla_ctx",
    )(q, q, kv, kv, krd)

    tq = _pick(s_lat, Q_TILE)
    nq = s_lat // tq
    full = lambda b, h, i: (0, 0)
    o = pl.pallas_call(
        lambda o_in, *refs: _mla_lat_kernel(*refs, tq=tq, s_self=s_lat, s_cache=past),
        grid=(bs, hp, nq),
        in_specs=[pl.BlockSpec(memory_space=pl.ANY),
                  pl.BlockSpec((tq, w2), lambda b, h, i: (tc // tq + b * nq + i, h)),
                  pl.BlockSpec((tq, LANES), lambda b, h, i: (tc // tq + b * nq + i, 2 * hp + h)),
                  pl.BlockSpec((s_lat, w2), lambda b, h, i: (tc // s_lat + b, h)),
                  pl.BlockSpec((s_lat, w2), lambda b, h, i: (tc // s_lat + b, hp + h)),
                  pl.BlockSpec((s_lat, LANES), lambda b, h, i: (tc // s_lat + b, 0)),
                  pl.BlockSpec((past, w2), lambda b, h, i: (t // past + b, h)),
                  pl.BlockSpec((past, w2), lambda b, h, i: (t // past + b, hp + h)),
                  pl.BlockSpec((None, past, LANES), lambda b, h, i: (b, 0, 0)),
                  pl.BlockSpec((s_lat, LANES), full), pl.BlockSpec((s_lat, LANES), full),
                  pl.BlockSpec((s_lat, LANES), full)],
        out_specs=pl.BlockSpec((tq, w2), lambda b, h, i: (tc // tq + b * nq + i, h)),
        out_shape=o_shape,
        scratch_shapes=[pltpu.VMEM((2, s_lat + past, w2), bf16), pltpu.VMEM((s_lat + past, w2), bf16)],
        input_output_aliases={0: 0},
        compiler_params=_cp("parallel", "parallel", "arbitrary"),
        name="attn_mla_lat",
    )(o, q, q, kv, kv, krd, kv, kv, krd_cache, cos, sup, sdn)
    return o


def _router_kernel(h_ref, w_ref, b_ref, eid_ref, wgt_ref):
    logits = jnp.dot(h_ref[...], w_ref[...], preferred_element_type=f32) + b_ref[...]
    lane = lax.broadcasted_iota(jnp.int32, logits.shape, 1)
    big = jnp.int32(LANES)
    neg = -jnp.inf

    def top(vals):
        v = jnp.max(vals, axis=-1, keepdims=True)
        return v, jnp.min(jnp.where(vals == v, lane, big), axis=-1, keepdims=True)

    gl = jnp.where(lane < N_GROUPS, logits, neg)
    gmax, g_idx = top(gl)
    p_group = 1.0 / jnp.sum(jnp.exp(gl - gmax), axis=-1, keepdims=True)
    lo = N_GROUPS + g_idx * EXPERTS_PER_GROUP
    el = jnp.where((lane >= lo) & (lane < lo + EXPERTS_PER_GROUP), logits, neg)
    v1, i1 = top(el)
    v2, i2 = top(jnp.where(lane == i1, neg, el))
    e = jnp.exp(v2 - v1)
    w1 = p_group / (1.0 + e)
    eid_ref[...] = jnp.where(lane == 0, i1 - N_GROUPS, jnp.where(lane == 1, i2 - N_GROUPS, 0))
    wgt_ref[...] = jnp.where(lane == 0, w1, jnp.where(lane == 1, w1 * e, 0.0))


def _router(h, w_r, b_r):
    t, d = h.shape
    tm = _pick(t, ROW_TILE)
    return pl.pallas_call(
        _router_kernel,
        grid=(t // tm,),
        in_specs=[pl.BlockSpec((tm, d), lambda i: (i, 0)),
                  pl.BlockSpec((d, LANES), lambda i: (0, 0)),
                  pl.BlockSpec((1, LANES), lambda i: (0, 0))],
        out_specs=[pl.BlockSpec((tm, LANES), lambda i: (i, 0)), pl.BlockSpec((tm, LANES), lambda i: (i, 0))],
        out_shape=[jax.ShapeDtypeStruct((t, LANES), jnp.int32), jax.ShapeDtypeStruct((t, LANES), f32)],
        compiler_params=_cp("parallel"),
        name="router",
    )(h, w_r, b_r)


def _moe_kernel(te_ref, nu_ref, tok_ref, dst_ref, hp_ref, rw_ref, w1_ref, w3_ref, w2_ref, y_ref,
                xg0, xg1, yb0, yb1, gsem, ssem, *, tm):
    del te_ref
    i = pl.program_id(0)
    n_used = nu_ref[0]
    xg, yb = (xg0, xg1), (yb0, yb1)

    def gather_copy(row, r, slot):
        return pltpu.make_async_copy(hp_ref.at[pl.ds(row, 1)], xg[slot].at[pl.ds(r, 1)], gsem.at[slot])

    def scatter_copy(row, r, slot):
        return pltpu.make_async_copy(yb[slot].at[pl.ds(r, 1)], y_ref.at[pl.ds(row, 1)], ssem.at[slot])

    def gather_wait(slot):
        pltpu.make_async_copy(hp_ref.at[pl.ds(0, tm)], xg[slot], gsem.at[slot]).wait()

    def scatter_wait(slot):
        pltpu.make_async_copy(yb[slot], y_ref.at[pl.ds(0, tm)], ssem.at[slot]).wait()

    def compute(slot):
        u = xg[slot][...]
        half = u.shape[1]
        x_lo = lax.bitcast_convert_type(u << 16, f32).astype(bf16)
        x_hi = lax.bitcast_convert_type(u & jnp.uint32(0xFFFF0000), f32).astype(bf16)
        a = (jnp.dot(x_lo, w1_ref[0:half, :], preferred_element_type=f32)
             + jnp.dot(x_hi, w1_ref[half:2 * half, :], preferred_element_type=f32))
        b = (jnp.dot(x_lo, w3_ref[0:half, :], preferred_element_type=f32)
             + jnp.dot(x_hi, w3_ref[half:2 * half, :], preferred_element_type=f32))
        hid = (jax.nn.silu(a) * b * rw_ref[...]).astype(bf16)
        yb[slot][...] = jnp.dot(hid, w2_ref[...], preferred_element_type=f32)

    @pl.when(i == 0)
    def _():
        yb1[...] = jnp.zeros_like(yb1)

        def prime(r, carry):
            gather_copy(tok_ref[r], r, 0).start()
            return carry

        lax.fori_loop(0, tm, prime, 0)

    for parity in (0, 1):
        cur, nxt = parity, 1 - parity

        @pl.when((i % 2 == parity) & (i >= 1) & (i <= n_used))
        def _():
            scatter_wait(cur)

        @pl.when((i % 2 == parity) & (i < n_used))
        def _():
            gather_wait(cur)
            for r in range(tm):
                gather_copy(tok_ref[(i + 1) * tm + r], r, nxt).start()
            for r in range(tm):
                scatter_copy(dst_ref[i * tm + r], r, nxt).start()

        @pl.when((i % 2 == parity) & (i + 1 <= n_used))
        def _():
            compute(cur)

        @pl.when((i % 2 == parity) & (i == n_used))
        def _():
            gather_wait(cur)

            def flush(r, carry):
                scatter_copy(dst_ref[i * tm + r], r, nxt).start()
                return carry

            lax.fori_loop(0, tm, flush, 0)
            scatter_wait(nxt)


def _moe(hp, eid, wgt, w1, w3, w2):
    t, half = hp.shape
    d = 2 * half
    n_exp, _, f = w1.shape
    tm = MOE_TILE
    n_assign = 2 * t
    n_tiles = n_assign // tm + n_exp + 1
    n_rows = n_tiles * tm

    e_flat = eid.T.reshape(-1)
    w_flat = wgt.T.reshape(-1)
    order = jnp.argsort(e_flat, stable=True).astype(jnp.int32)
    counts = jnp.sum((e_flat[None, :] == jnp.arange(n_exp, dtype=jnp.int32)[:, None]).astype(jnp.int32), axis=1)
    padded = ((counts + tm - 1) // tm) * tm
    pad_end = jnp.cumsum(padded)
    pad_start = pad_end - padded
    src_start = jnp.cumsum(counts) - counts
    n_used = (pad_end[-1] // tm).astype(jnp.int32).reshape(1)
    tile_start = jnp.arange(n_tiles, dtype=jnp.int32) * tm
    tile_e = jnp.minimum(jnp.sum((pad_end[None, :] <= tile_start[:, None]).astype(jnp.int32), axis=1), n_exp - 1)
    rows = jnp.arange(n_rows, dtype=jnp.int32)
    row_e = jnp.broadcast_to(tile_e[:, None], (n_tiles, tm)).reshape(-1)
    j = rows - pad_start[row_e]
    valid = (j < counts[row_e]) & (rows < pad_end[-1])
    a = order[jnp.clip(src_start[row_e] + j, 0, n_assign - 1)]
    spare = n_assign + rows % tm
    row_tok = jnp.where(valid, a % t, 0).astype(jnp.int32)
    row_dst = jnp.concatenate([spare[:tm], jnp.where(valid, a, spare)]).astype(jnp.int32)
    row_w = jnp.where(valid, w_flat[a], 0.0).astype(f32).reshape(n_rows, 1)

    return pl.pallas_call(
        functools.partial(_moe_kernel, tm=tm),
        grid_spec=pltpu.PrefetchScalarGridSpec(
            num_scalar_prefetch=4,
            grid=(n_tiles,),
            in_specs=[pl.BlockSpec(memory_space=pl.ANY),
                      pl.BlockSpec((tm, 1), lambda i, te, nu, tok, dst: (i, 0)),
                      pl.BlockSpec((None, d, f), lambda i, te, nu, tok, dst: (te[i], 0, 0)),
                      pl.BlockSpec((None, d, f), lambda i, te, nu, tok, dst: (te[i], 0, 0)),
                      pl.BlockSpec((None, f, d), lambda i, te, nu, tok, dst: (te[i], 0, 0))],
            out_specs=pl.BlockSpec(memory_space=pl.ANY),
            scratch_shapes=[pltpu.VMEM((tm, half), jnp.uint32), pltpu.VMEM((tm, half), jnp.uint32),
                            pltpu.VMEM((tm, d), f32), pltpu.VMEM((tm, d), f32),
                            pltpu.SemaphoreType.DMA((2,)), pltpu.SemaphoreType.DMA((2,))],
        ),
        out_shape=jax.ShapeDtypeStruct((n_assign + tm, d), f32),
        compiler_params=_cp("arbitrary"),
        name="moe",
    )(tile_e, n_used, row_tok, row_dst, hp, row_w, w1, w3, w2)


def kernel(x_prompt, x_sample, c, c_ctx, cache_a_k, cache_a_v, cache_b_k, cache_b_v, cache_c_kv, cache_c_kr,
           g_norm1, g_norm2, g_final, w_ada, b_ada, w_in_ab, w_out_ab, lam_q1, lam_k1, lam_q2, lam_k2,
           g_sub_a, rpb_b, w_down_c, g_q_c, g_kv_c, w_uq_c, w_uk_c, w_uv_c, w_out_c,
           w_group_router, b_group_router, w_expert_router, b_expert_router, w1_moe, w3_moe, w2_moe):
    bc, seq, d = x_prompt.shape
    bs, s_lat, _ = x_sample.shape
    past = cache_a_k.shape[2]
    depth = w_ada.shape[0]
    tc, ts = bc * seq, bs * s_lat
    t = tc + ts
    h_a, h_b = cache_a_k.shape[3], cache_b_k.shape[3]
    ab = (h_a + h_b) * HEAD_DIM
    dims = (bc, seq, bs, s_lat, past)
    assert tc % s_lat == 0 and t % past == 0 and s_lat % (2 * ROW_TILE) == 0

    x = jnp.concatenate([x_prompt.reshape(tc, d), x_sample.reshape(ts, d)], axis=0)
    cond = jnp.concatenate([c_ctx[None], c, jnp.zeros((COND_ROWS_PAD - 1 - bs, d), f32)], axis=0)
    mod = _ada_mod(cond, w_ada, b_ada)
    tables = _rope_tables(s_lat)

    n_exp = N_GROUPS * EXPERTS_PER_GROUP
    sak, sav, sbk, sbv, sckv, sckr = [], [], [], [], [], []
    moe_pending = None
    for l in range(depth):
        mod4 = mod[l].reshape(COND_ROWS_PAD, 6, 1, d)
        if moe_pending is None:
            h = _norm(x, g_norm1[l], mod4, tc, s_lat, sel=0)[0]
        else:
            x, h = _norm(x, g_norm1[l], mod4, tc, s_lat, sel=0, moe=moe_pending)
        if l % 2 == 0:
            e = l // 2
            lam_init = 0.8 - 0.6 * math.exp(-0.3 * l)
            lam = (jnp.exp(jnp.sum((lam_q1[e] * lam_k1[e]).astype(f32)))
                   - jnp.exp(jnp.sum((lam_q2[e] * lam_k2[e]).astype(f32))) + lam_init)
            p = _matmul(h, w_in_ab[e].astype(bf16), f32)
            wa = h_a * HEAD_DIM
            sak.append(p[:tc, wa:2 * wa].reshape(bc, seq, h_a, HEAD_DIM))
            sav.append(p[:tc, 2 * wa:3 * wa].reshape(bc, seq, h_a, HEAD_DIM))
            wb = h_b * HEAD_DIM
            sbk.append(p[:tc, 3 * wa + wb:3 * wa + 2 * wb].reshape(bc, seq, h_b, HEAD_DIM))
            sbv.append(p[:tc, 3 * wa + 2 * wb:].reshape(bc, seq, h_b, HEAD_DIM))
            o = _attn_even(p, ab, lam, lam_init, g_sub_a[e], rpb_b[e],
                           cache_a_k[:, e].reshape(bs, past, wa), cache_a_v[:, e].reshape(bs, past, wa),
                           cache_b_k[:, e].reshape(bs, past, wb), cache_b_v[:, e].reshape(bs, past, wb),
                           dims, tables)
            w_out = w_out_ab[e].astype(bf16)
        else:
            oi = l // 2
            q_lora, kv_lora = g_q_c.shape[1], g_kv_c.shape[1]
            h_c = w_uk_c.shape[2] // NOPE_DIM
            wd = w_down_c[oi]
            wd = jnp.concatenate([wd, wd[:, q_lora + kv_lora:]], axis=1).astype(bf16)
            cq, ckv, krd = _mla_down(h, wd, g_q_c[oi], g_kv_c[oi])
            sckv.append(ckv[:tc].reshape(bc, seq, kv_lora))
            sckr.append(krd[:tc, :ROPE_DIM].reshape(bc, seq, ROPE_DIM))
            wq = w_uq_c[oi].reshape(q_lora, h_c, NOPE_DIM + ROPE_DIM)
            wq = jnp.concatenate([wq[:, :, :NOPE_DIM].reshape(q_lora, -1), wq[:, :, NOPE_DIM:].reshape(q_lora, -1)],
                                 axis=1).astype(bf16)
            q = _matmul(cq, wq, f32)
            ckv_all = jnp.concatenate([ckv, cache_c_kv[:, oi].reshape(bs * past, kv_lora)], axis=0).astype(bf16)
            wkv = jnp.concatenate([w_uk_c[oi], w_uv_c[oi]], axis=1).astype(bf16)
            kv = _matmul(ckv_all, wkv, bf16)
            krc = cache_c_kr[:, oi]
            o = _attn_mla(q, kv, krd, jnp.concatenate([krc, krc], axis=-1), h_c, dims, tables)
            w_out = w_out_c[oi].astype(bf16)
        x = _matmul(o, w_out, f32, residual=(x, mod4, 2, tc, s_lat))
        h2, hp = _norm(x, g_norm2[l], mod4, tc, s_lat, sel=1, packed=True)
        w_r = jnp.concatenate([w_group_router[l], w_expert_router[l]], axis=1)
        w_r = jnp.pad(w_r, ((0, 0), (0, LANES - w_r.shape[1]))).astype(bf16)
        b_r = jnp.pad(jnp.concatenate([b_group_router[l], b_expert_router[l]]), (0, LANES - N_GROUPS - n_exp))
        eid, wgt = _router(h2, w_r, b_r.reshape(1, LANES).astype(f32))
        f = w1_moe.shape[-1]
        y = _moe(hp, eid[:, :2], wgt[:, :2],
                 w1_moe[l].reshape(n_exp, d, f).astype(bf16), w3_moe[l].reshape(n_exp, d, f).astype(bf16),
                 w2_moe[l].reshape(n_exp, f, d).astype(bf16))
        moe_pending = (y, mod4, 5)
    out = _norm(x, g_final, None, tc, s_lat, moe=moe_pending, final=True)[0]
    y_prompt = out[:tc].reshape(bc, seq, d)
    y_sample = out[tc:].reshape(bs, s_lat, d)
    return (y_prompt, y_sample, jnp.stack(sak, axis=1), jnp.stack(sav, axis=1), jnp.stack(sbk, axis=1),
            jnp.stack(sbv, axis=1), jnp.stack(sckv, axis=1), jnp.stack(sckr, axis=1))
```

```python
import functools
import math

import numpy as np
import jax
import jax.numpy as jnp
from jax import lax
from jax.experimental import pallas as pl
from jax.experimental.pallas import tpu as pltpu

GRID_W = 64
HEAD_DIM = 128
WIN_ROWS = 8
WIN_COLS = 16
NOPE_DIM = 128
ROPE_DIM = 64
V_DIM = 128
N_GROUPS = 4
EXPERTS_PER_GROUP = 8
ROPE_THETA = 10000.0
NORM_EPS = 1e-6
NEG_INF = -1e30
LOG2E = math.log2(math.e)

LANES = 128
COND_ROWS_PAD = 16
VMEM_LIMIT = 56 * 1024 * 1024
ROW_TILE = 512
COL_TILE = 512
NORM_TILE = 256
MOE_TILE = 256
Q_TILE = 256
NBR_ROWS = 4
CTX_HEADS = 8

f32 = jnp.float32
bf16 = jnp.bfloat16


def _cp(*sem):
    return pltpu.CompilerParams(dimension_semantics=sem, vmem_limit_bytes=VMEM_LIMIT)


def _pick(n, pref):
    if n <= pref:
        return n
    t = pref
    while n % t:
        t //= 2
    return t


def _cond_row(r0, tc, s_lat):
    return jnp.where(r0 < tc, 0, 1 + (r0 - tc) // s_lat)


def _dot_nt(a, b):
    return lax.dot_general(a, b, (((1,), (1,)), ((), ())), preferred_element_type=f32)


def _ada_kernel(c_ref, w_ref, b_ref, o_ref):
    s = jax.nn.silu(c_ref[...]).astype(bf16)
    o_ref[...] = jnp.dot(s, w_ref[...].astype(bf16), preferred_element_type=f32) + b_ref[...]


def _ada_mod(cond, w_ada, b_ada):
    depth, d, n = w_ada.shape
    r = cond.shape[0]
    tn = _pick(n, COL_TILE)
    return pl.pallas_call(
        _ada_kernel,
        grid=(depth, n // tn),
        in_specs=[pl.BlockSpec((r, d), lambda l, j: (0, 0)),
                  pl.BlockSpec((None, d, tn), lambda l, j: (l, 0, j)),
                  pl.BlockSpec((None, 1, tn), lambda l, j: (l, 0, j))],
        out_specs=pl.BlockSpec((None, r, tn), lambda l, j: (l, 0, j)),
        out_shape=jax.ShapeDtypeStruct((depth, r, n), f32),
        compiler_params=_cp("parallel", "parallel"),
        name="ada_mod",
    )(cond, w_ada, b_ada.reshape(depth, 1, n))


def _norm_kernel(*refs, combine, final, packed):
    it = iter(refs)
    x_ref = next(it)
    if combine:
        y0_ref, y1_ref, gate_ref = next(it), next(it), next(it)
    g_ref = next(it)
    if not final:
        shift_ref, scale_ref = next(it), next(it)
    x = x_ref[...]
    if combine:
        x = x + gate_ref[...] * (y0_ref[...] + y1_ref[...])
        if not final:
            next(it)[...] = x
    y = x * lax.rsqrt(jnp.mean(x * x, axis=-1, keepdims=True) + NORM_EPS) * g_ref[...]
    if final:
        next(it)[...] = y
        return
    h = (y * (1.0 + scale_ref[...]) + shift_ref[...]).astype(bf16)
    next(it)[...] = h
    if packed:
        bits = lax.bitcast_convert_type(h.astype(f32), jnp.uint32)
        half = bits.shape[1] // 2
        next(it)[...] = (bits[:, half:] & jnp.uint32(0xFFFF0000)) | (bits[:, :half] >> 16)


def _norm(x, g, mod4, tc, s_lat, *, sel=None, moe=None, final=False, packed=False):
    t, d = x.shape
    tm = NORM_TILE
    nt = t // tm
    combine = moe is not None
    row = lambda i: (i, 0)
    modspec = lambda j: pl.BlockSpec((None, None, 1, d), lambda i: (_cond_row(i * tm, tc, s_lat), j, 0, 0))
    args, specs = [x], [pl.BlockSpec((tm, d), row)]
    if combine:
        y, mod4_moe, gate_j = moe
        args += [y, y, mod4_moe]
        specs += [pl.BlockSpec((tm, d), row), pl.BlockSpec((tm, d), lambda i: (i + nt, 0)), modspec(gate_j)]
    args.append(g.reshape(1, d))
    specs.append(pl.BlockSpec((1, d), lambda i: (0, 0)))
    if not final:
        args += [mod4, mod4]
        specs += [modspec(3 * sel), modspec(3 * sel + 1)]
    shapes, ospecs = [], []
    if combine and not final:
        shapes.append(jax.ShapeDtypeStruct((t, d), f32))
        ospecs.append(pl.BlockSpec((tm, d), row))
    if final:
        shapes.append(jax.ShapeDtypeStruct((t, d), f32))
        ospecs.append(pl.BlockSpec((tm, d), row))
    else:
        shapes.append(jax.ShapeDtypeStruct((t, d), bf16))
        ospecs.append(pl.BlockSpec((tm, d), row))
        if packed:
            shapes.append(jax.ShapeDtypeStruct((t, d // 2), jnp.uint32))
            ospecs.append(pl.BlockSpec((tm, d // 2), row))
    return pl.pallas_call(
        functools.partial(_norm_kernel, combine=combine, final=final, packed=packed),
        grid=(nt,),
        in_specs=specs,
        out_specs=ospecs,
        out_shape=shapes,
        input_output_aliases={0: 0} if (combine and not final) else {},
        compiler_params=_cp("parallel"),
        name="norm",
    )(*args)


def _mm_kernel(*refs, residual):
    if residual:
        a_ref, b_ref, x_ref, gate_ref, o_ref = refs
    else:
        a_ref, b_ref, o_ref = refs
    acc = jnp.dot(a_ref[...], b_ref[...], preferred_element_type=f32)
    if residual:
        acc = x_ref[...] + gate_ref[...] * acc
    o_ref[...] = acc.astype(o_ref.dtype)


def _matmul(a, b, out_dtype, *, residual=None):
    m, k = a.shape
    n = b.shape[1]
    tm = _pick(m, 2 * ROW_TILE if (k >= 4 * ROW_TILE or k <= ROW_TILE) else ROW_TILE)
    tn = _pick(n, COL_TILE * max(1, min(4, 4 * ROW_TILE // k)))
    args = [a, b]
    specs = [pl.BlockSpec((tm, k), lambda i, j: (i, 0)), pl.BlockSpec((k, tn), lambda i, j: (0, j))]
    aliases = {}
    if residual is not None:
        x, mod4, gate_j, tc, s_lat = residual
        args += [x, mod4]
        specs += [pl.BlockSpec((tm, tn), lambda i, j: (i, j)),
                  pl.BlockSpec((None, None, 1, tn), lambda i, j: (_cond_row(i * tm, tc, s_lat), gate_j, 0, j))]
        aliases = {2: 0}
    return pl.pallas_call(
        functools.partial(_mm_kernel, residual=residual is not None),
        grid=(m // tm, n // tn),
        in_specs=specs,
        out_specs=pl.BlockSpec((tm, tn), lambda i, j: (i, j)),
        out_shape=jax.ShapeDtypeStruct((m, n), out_dtype),
        input_output_aliases=aliases,
        compiler_params=_cp("parallel", "parallel"),
        name="matmul",
    )(*args)


def _down_kernel(a_ref, w_ref, gq_ref, gkv_ref, cq_ref, ckv_ref, kr_ref, *, q_lora, kv_lora):
    acc = jnp.dot(a_ref[...], w_ref[...], preferred_element_type=f32)
    cq = acc[:, :q_lora]
    cq = cq * lax.rsqrt(jnp.mean(cq * cq, axis=-1, keepdims=True) + NORM_EPS) * gq_ref[...]
    cq_ref[...] = cq.astype(bf16)
    ckv = acc[:, q_lora:q_lora + kv_lora]
    ckv_ref[...] = ckv * lax.rsqrt(jnp.mean(ckv * ckv, axis=-1, keepdims=True) + NORM_EPS) * gkv_ref[...]
    kr_ref[...] = acc[:, q_lora + kv_lora:]


def _mla_down(h, w_down_dup, g_q, g_kv):
    t, d = h.shape
    q_lora, kv_lora = g_q.shape[0], g_kv.shape[0]
    n = w_down_dup.shape[1]
    tm = _pick(t, ROW_TILE)
    return pl.pallas_call(
        functools.partial(_down_kernel, q_lora=q_lora, kv_lora=kv_lora),
        grid=(t // tm,),
        in_specs=[pl.BlockSpec((tm, d), lambda i: (i, 0)),
                  pl.BlockSpec((d, n), lambda i: (0, 0)),
                  pl.BlockSpec((1, q_lora), lambda i: (0, 0)),
                  pl.BlockSpec((1, kv_lora), lambda i: (0, 0))],
        out_specs=[pl.BlockSpec((tm, q_lora), lambda i: (i, 0)),
                   pl.BlockSpec((tm, kv_lora), lambda i: (i, 0)),
                   pl.BlockSpec((tm, LANES), lambda i: (i, 0))],
        out_shape=[jax.ShapeDtypeStruct((t, q_lora), bf16),
                   jax.ShapeDtypeStruct((t, kv_lora), f32),
                   jax.ShapeDtypeStruct((t, LANES), f32)],
        compiler_params=_cp("parallel"),
        name="mla_down",
    )(h, w_down_dup, g_q.reshape(1, -1), g_kv.reshape(1, -1))


def _rope_tables(s_lat):
    half = ROPE_DIM // 2
    quarter = half // 2
    tpos = jnp.arange(s_lat)
    inv = ROPE_THETA ** (-jnp.arange(0, half, 2, dtype=f32) / half)
    ang_r = (tpos // GRID_W).astype(f32)[:, None] * inv
    ang_c = (tpos % GRID_W).astype(f32)[:, None] * inv
    ang = jnp.concatenate([ang_r, ang_r, ang_c, ang_c], axis=-1)
    ang = jnp.tile(ang, (1, LANES // ROPE_DIM))
    first = (jnp.arange(LANES) % half) < quarter
    sin = jnp.sin(ang)
    return jnp.cos(ang), jnp.where(first, -sin, 0.0), jnp.where(first, 0.0, sin)


def _rope(x, cos, sin_up, sin_dn):
    quarter = ROPE_DIM // 4
    return x * cos + pltpu.roll(x, LANES - quarter, 1) * sin_up + pltpu.roll(x, quarter, 1) * sin_dn


def _softmax_parts(s):
    m = jnp.max(s, axis=-1, keepdims=True)
    e = jnp.exp2(s - m)
    return e, jnp.sum(e, axis=-1, keepdims=True)


def _diff_logits(q, kk):
    dqk = HEAD_DIM // 2
    q = q * (dqk ** -0.5 * LOG2E)
    lane = lax.broadcasted_iota(jnp.int32, q.shape, 1)
    return (_dot_nt(jnp.where(lane < dqk, q, 0.0).astype(bf16), kk),
            _dot_nt(jnp.where(lane >= dqk, q, 0.0).astype(bf16), kk))


def _diff_finish(s1, m1, s2, m2, vv, lam, g, lam_init):
    e1, e2 = jnp.exp2(s1 - m1), jnp.exp2(s2 - m2)
    l1, l2 = jnp.sum(e1, axis=-1, keepdims=True), jnp.sum(e2, axis=-1, keepdims=True)
    a = e1 * (1.0 / l1) - e2 * (lam / l2)
    o = jnp.dot(a.astype(bf16), vv, preferred_element_type=f32)
    o = o * lax.rsqrt(jnp.mean(o * o, axis=-1, keepdims=True) + NORM_EPS) * g
    return (o * (1.0 - lam_init)).astype(bf16)


def _diff_ctx_kernel(lam_ref, q_ref, k_ref, v_ref, g_ref, o_ref, *, nh, lam_init):
    for hh in range(nh):
        cols = slice(hh * HEAD_DIM, (hh + 1) * HEAD_DIM)
        s1, s2 = _diff_logits(q_ref[:, cols], k_ref[:, cols].astype(bf16))
        o_ref[:, cols] = _diff_finish(s1, jnp.max(s1, axis=-1, keepdims=True), s2, jnp.max(s2, axis=-1, keepdims=True),
                                      v_ref[:, cols].astype(bf16), lam_ref[0], g_ref[...], lam_init)


def _diff_lat_kernel(lam_ref, q_ref, q_nx_ref, k_ref, v_ref, kc_ref, vc_ref, cos_ref, sup_ref, sdn_ref, g_ref, o_ref,
                     kall, vall, s0, s1, m0, m1, *, tq, s_self, s_cache, nq, lam_init):
    qi = pl.program_id(2)
    bufs = ((s0, m0), (s1, m1))

    def logits(q_r, tile, s_buf, m_buf):
        rows = pl.ds(pl.multiple_of(tile * tq, tq), tq)
        q = _rope(q_r[...], cos_ref[rows, :], sup_ref[rows, :], sdn_ref[rows, :])
        for c, s in enumerate(_diff_logits(q, kall[...])):
            s_buf[c] = s
            m_buf[c] = jnp.max(s, axis=-1, keepdims=True)

    @pl.when(qi == 0)
    def _():
        kall[0:s_self, :] = _rope(k_ref[...], cos_ref[...], sup_ref[...], sdn_ref[...]).astype(bf16)
        kall[s_self:s_self + s_cache, :] = kc_ref[...].astype(bf16)
        vall[0:s_self, :] = v_ref[...].astype(bf16)
        vall[s_self:s_self + s_cache, :] = vc_ref[...].astype(bf16)
        logits(q_ref, 0, s0, m0)

    for parity in (0, 1):
        (s_cur, m_cur), (s_nxt, m_nxt) = bufs[parity], bufs[1 - parity]

        @pl.when(qi % 2 == parity)
        def _():
            logits(q_nx_ref, jnp.minimum(qi + 1, nq - 1), s_nxt, m_nxt)
            o_ref[...] = _diff_finish(s_cur[0], m_cur[0], s_cur[1], m_cur[1], vall[...], lam_ref[0], g_ref[...],
                                      lam_init)


def _soft_ctx_kernel(q_ref, k_ref, v_ref, o_ref, *, nh):
    for hh in range(nh):
        cols = slice(hh * HEAD_DIM, (hh + 1) * HEAD_DIM)
        q = (q_ref[:, cols] * (HEAD_DIM ** -0.5 * LOG2E)).astype(bf16)
        e, l = _softmax_parts(_dot_nt(q, k_ref[:, cols].astype(bf16)))
        o = jnp.dot(e.astype(bf16), v_ref[:, cols].astype(bf16), preferred_element_type=f32)
        o_ref[:, cols] = (o * (1.0 / l)).astype(bf16)


def _nbr_kernel(ks_ref, cfg_ref, q_ref, k_ref, v_ref, kc_ref, vc_ref, bias_ref, o_ref, *, kw):
    del cfg_ref
    t = pl.program_id(2)
    rows = pl.ds(pl.multiple_of(ks_ref[t] * GRID_W, GRID_W), kw)
    q = (q_ref[...] * (HEAD_DIM ** -0.5 * LOG2E)).astype(bf16)
    s_loc = _dot_nt(q, k_ref[rows, :].astype(bf16)) + bias_ref[...]
    s_ctx = _dot_nt(q, kc_ref[...].astype(bf16))
    m = jnp.maximum(jnp.max(s_loc, axis=-1, keepdims=True), jnp.max(s_ctx, axis=-1, keepdims=True))
    e_loc, e_ctx = jnp.exp2(s_loc - m), jnp.exp2(s_ctx - m)
    l = jnp.sum(e_loc, axis=-1, keepdims=True) + jnp.sum(e_ctx, axis=-1, keepdims=True)
    o = (jnp.dot(e_loc.astype(bf16), v_ref[rows, :].astype(bf16), preferred_element_type=f32)
         + jnp.dot(e_ctx.astype(bf16), vc_ref[...].astype(bf16), preferred_element_type=f32))
    o_ref[...] = (o * (1.0 / l)).astype(bf16)


def _mla_head(qn, qr_masked, kcat, vv):
    qc = (jnp.concatenate([qn, qr_masked], axis=-1) * ((NOPE_DIM + ROPE_DIM) ** -0.5 * LOG2E)).astype(bf16)
    e, l = _softmax_parts(_dot_nt(qc, kcat))
    o = jnp.dot(e.astype(bf16), vv, preferred_element_type=f32)
    return (o * (1.0 / l)).astype(bf16)


def _mla_ctx_kernel(qn_ref, qr_ref, kn_ref, v_ref, kr_ref, o_ref, *, npair):
    hd = NOPE_DIM
    kr = kr_ref[...].astype(bf16)
    lane = lax.broadcasted_iota(jnp.int32, (qr_ref.shape[0], LANES), 1)
    for pp in range(npair):
        qr = qr_ref[:, pp * LANES:(pp + 1) * LANES]
        for hh in range(2):
            c = slice((2 * pp + hh) * hd, (2 * pp + hh + 1) * hd)
            mask = (lane < ROPE_DIM) if hh == 0 else (lane >= ROPE_DIM)
            kcat = jnp.concatenate([kn_ref[:, c], kr], axis=-1)
            o_ref[:, c] = _mla_head(qn_ref[:, c], jnp.where(mask, qr, 0.0), kcat, v_ref[:, c])


def _mla_lat_kernel(qn_ref, qr_ref, qn_nx_ref, qr_nx_ref, kn_ref, v_ref, kr_ref, knc_ref, vc_ref, krc_ref,
                    cos_ref, sup_ref, sdn_ref, o_ref, kcat, vaug, s0, s1, m0, m1, *, tq, s_self, s_cache, nq):
    qi = pl.program_id(2)
    hd = NOPE_DIM
    s_all = s_self + s_cache
    bufs = ((s0, m0), (s1, m1))

    def logits(qn_r, qr_r, tile, s_buf, m_buf):
        rows = pl.ds(pl.multiple_of(tile * tq, tq), tq)
        qr = _rope(qr_r[...], cos_ref[rows, :], sup_ref[rows, :], sdn_ref[rows, :])
        lane = lax.broadcasted_iota(jnp.int32, qr.shape, 1)
        for hh in range(2):
            mask = (lane < ROPE_DIM) if hh == 0 else (lane >= ROPE_DIM)
            qc = jnp.concatenate([qn_r[:, hh * hd:(hh + 1) * hd], jnp.where(mask, qr, 0.0)], axis=-1)
            s = _dot_nt((qc * ((NOPE_DIM + ROPE_DIM) ** -0.5 * LOG2E)).astype(bf16), kcat[hh])
            s_buf[hh] = s
            m_buf[hh] = jnp.max(s, axis=-1, keepdims=True)

    @pl.when(qi == 0)
    def _():
        kr = _rope(kr_ref[...], cos_ref[...], sup_ref[...], sdn_ref[...]).astype(bf16)
        krc = krc_ref[...].astype(bf16)
        ones = jnp.ones((s_all, V_DIM), bf16)
        for hh in range(2):
            c = slice(hh * hd, (hh + 1) * hd)
            kcat[hh, 0:s_self, 0:hd] = kn_ref[:, c]
            kcat[hh, 0:s_self, hd:2 * hd] = kr
            kcat[hh, s_self:s_all, 0:hd] = knc_ref[:, c]
            kcat[hh, s_self:s_all, hd:2 * hd] = krc
            vaug[hh, 0:s_self, 0:V_DIM] = v_ref[:, c]
            vaug[hh, s_self:s_all, 0:V_DIM] = vc_ref[:, c]
            vaug[hh, :, V_DIM:2 * V_DIM] = ones
        logits(qn_ref, qr_ref, 0, s0, m0)

    for parity in (0, 1):
        (s_cur, m_cur), (s_nxt, m_nxt) = bufs[parity], bufs[1 - parity]

        @pl.when(qi % 2 == parity)
        def _():
            logits(qn_nx_ref, qr_nx_ref, jnp.minimum(qi + 1, nq - 1), s_nxt, m_nxt)
            for hh in range(2):
                e = jnp.exp2(s_cur[hh] - m_cur[hh]).astype(bf16)
                oa = jnp.dot(e, vaug[hh], preferred_element_type=f32)
                o_ref[:, hh * V_DIM:(hh + 1) * V_DIM] = (oa[:, :V_DIM] * (1.0 / oa[:, V_DIM:V_DIM + 1])).astype(bf16)


def _nbr_tables(rows):
    r_tile = min(NBR_ROWS, rows)
    kr = min(WIN_ROWS, rows)
    krw = min(rows, r_tile + kr)
    n_rel = 2 * WIN_ROWS - 1
    ks_list, cfg_list, cfgs = [], [], {}
    for t in range(rows // r_tile):
        qr = t * r_tile + np.arange(r_tile)
        r0 = np.clip(qr - kr // 2, 0, rows - kr)
        ks = int(np.clip(t * r_tile - kr // 2, 0, rows - krw))
        krow = ks + np.arange(krw)
        row_ok = (krow[None, :] >= r0[:, None]) & (krow[None, :] < r0[:, None] + kr)
        rel = np.where(row_ok, krow[None, :] - qr[:, None] + WIN_ROWS - 1, n_rel).astype(np.int32)
        key = rel.tobytes()
        if key not in cfgs:
            cfgs[key] = (len(cfgs), rel)
        ks_list.append(ks)
        cfg_list.append(cfgs[key][0])
    rel_all = np.stack([c[1] for c in sorted(cfgs.values(), key=lambda c: c[0])])
    return r_tile, krw, np.asarray(ks_list, np.int32), np.asarray(cfg_list, np.int32), rel_all


def _nbr_bias(rpb, rel_all):
    w = GRID_W
    kc = min(WIN_COLS, w)
    h, n_rel, _ = rpb.shape
    span = 2 * w - 1
    off = np.clip(np.arange(span) - (w - 1), -(WIN_COLS - 1), WIN_COLS - 1) + WIN_COLS - 1
    v = rpb[:, :, off].astype(f32) * LOG2E
    m = jnp.tile(v, (1, 1, w + 1))[:, :, :w * (span + 1)].reshape(h, n_rel, w, span + 1)[:, :, :, :w]
    toep = m[:, :, ::-1, :]
    cols = np.arange(w)
    col_start = np.clip(cols - kc // 2, 0, w - kc)
    col_ok = (cols[None, :] >= col_start[:, None]) & (cols[None, :] < col_start[:, None] + kc)
    toep = jnp.where(col_ok, toep, NEG_INF)
    toep = jnp.concatenate([toep, jnp.full((h, 1, w, w), NEG_INF, f32)], axis=1)
    n_cfg, r_tile, krw = rel_all.shape
    b = toep[:, rel_all.reshape(-1)].reshape(h, n_cfg, r_tile, krw, w, w)
    return b.transpose(0, 1, 2, 4, 3, 5).reshape(h, n_cfg, r_tile * w, krw * w)


def _attn_even(p, o_width, lam, lam_init, g_sub, rpb, ck_a, cv_a, ck_b, cv_b, dims, tables):
    bc, seq, bs, s_lat, past = dims
    t = p.shape[0]
    tc = bc * seq
    h_a = ck_a.shape[-1] // HEAD_DIM
    h_b = ck_b.shape[-1] // HEAD_DIM
    cos, sup, sdn = tables
    hd = HEAD_DIM
    lam_arr = jnp.reshape(lam, (1,)).astype(f32)
    g2 = g_sub.reshape(1, hd)
    smem = pl.BlockSpec(memory_space=pltpu.SMEM)
    o_shape = jax.ShapeDtypeStruct((t, o_width), bf16)

    nh = _pick(h_a, CTX_HEADS)
    ga = h_a // nh
    wb = nh * hd
    o = pl.pallas_call(
        functools.partial(_diff_ctx_kernel, nh=nh, lam_init=lam_init),
        grid=(bc, ga),
        in_specs=[smem,
                  pl.BlockSpec((seq, wb), lambda b, h: (b, h)),
                  pl.BlockSpec((seq, wb), lambda b, h: (b, ga + h)),
                  pl.BlockSpec((seq, wb), lambda b, h: (b, 2 * ga + h)),
                  pl.BlockSpec((1, hd), lambda b, h: (0, 0))],
        out_specs=pl.BlockSpec((seq, wb), lambda b, h: (b, h)),
        out_shape=o_shape,
        compiler_params=_cp("parallel", "parallel"),
        name="attn_diff_ctx",
    )(lam_arr, p, p, p, g2)

    nhb = _pick(h_b, CTX_HEADS)
    gb = h_b // nhb
    wbb = nhb * hd
    base = 3 * h_a * hd // wbb
    o = pl.pallas_call(
        lambda o_in, q, k, v, o_out: _soft_ctx_kernel(q, k, v, o_out, nh=nhb),
        grid=(bc, gb),
        in_specs=[pl.BlockSpec(memory_space=pl.ANY),
                  pl.BlockSpec((seq, wbb), lambda b, h: (b, base + h)),
                  pl.BlockSpec((seq, wbb), lambda b, h: (b, base + gb + h)),
                  pl.BlockSpec((seq, wbb), lambda b, h: (b, base + 2 * gb + h))],
        out_specs=pl.BlockSpec((seq, wbb), lambda b, h: (b, h_a * hd // wbb + h)),
        out_shape=o_shape,
        input_output_aliases={0: 0},
        compiler_params=_cp("parallel", "parallel"),
        name="attn_soft_ctx",
    )(o, p, p, p)

    tq = _pick(s_lat, Q_TILE)
    nq = s_lat // tq
    full = lambda b, h, i: (0, 0)
    o = pl.pallas_call(
        lambda o_in, *refs: _diff_lat_kernel(*refs, tq=tq, s_self=s_lat, s_cache=past, nq=nq, lam_init=lam_init),
        grid=(bs, h_a, nq),
        in_specs=[pl.BlockSpec(memory_space=pl.ANY), smem,
                  pl.BlockSpec((tq, hd), lambda b, h, i: (tc // tq + b * nq + i, h)),
                  pl.BlockSpec((tq, hd), lambda b, h, i: (tc // tq + b * nq + jnp.minimum(i + 1, nq - 1), h)),
                  pl.BlockSpec((s_lat, hd), lambda b, h, i: (tc // s_lat + b, h_a + h)),
                  pl.BlockSpec((s_lat, hd), lambda b, h, i: (tc // s_lat + b, 2 * h_a + h)),
                  pl.BlockSpec((None, past, hd), lambda b, h, i: (b, 0, h)),
                  pl.BlockSpec((None, past, hd), lambda b, h, i: (b, 0, h)),
                  pl.BlockSpec((s_lat, hd), full), pl.BlockSpec((s_lat, hd), full), pl.BlockSpec((s_lat, hd), full),
                  pl.BlockSpec((1, hd), full)],
        out_specs=pl.BlockSpec((tq, hd), lambda b, h, i: (tc // tq + b * nq + i, h)),
        out_shape=o_shape,
        scratch_shapes=[pltpu.VMEM((s_lat + past, hd), bf16), pltpu.VMEM((s_lat + past, hd), bf16),
                        pltpu.VMEM((2, tq, s_lat + past), f32), pltpu.VMEM((2, tq, s_lat + past), f32),
                        pltpu.VMEM((2, tq, 1), f32), pltpu.VMEM((2, tq, 1), f32)],
        input_output_aliases={0: 0},
        compiler_params=_cp("parallel", "parallel", "arbitrary"),
        name="attn_diff_lat",
    )(o, lam_arr, p, p, p, p, ck_a, cv_a, cos, sup, sdn, g2)

    rows = s_lat // GRID_W
    r_tile, krw, ks_np, cfg_np, rel_all = _nbr_tables(rows)
    tqn, kw = r_tile * GRID_W, krw * GRID_W
    nqn = rows // r_tile
    bias = _nbr_bias(rpb, rel_all)
    o = pl.pallas_call(
        lambda ks, cfg, o_in, *refs: _nbr_kernel(ks, cfg, *refs, kw=kw),
        grid_spec=pltpu.PrefetchScalarGridSpec(
            num_scalar_prefetch=2,
            grid=(bs, h_b, nqn),
            in_specs=[pl.BlockSpec(memory_space=pl.ANY),
                      pl.BlockSpec((tqn, hd), lambda b, h, i, ks, cfg: (tc // tqn + b * nqn + i, 3 * h_a + h)),
                      pl.BlockSpec((s_lat, hd), lambda b, h, i, ks, cfg: (tc // s_lat + b, 3 * h_a + h_b + h)),
                      pl.BlockSpec((s_lat, hd), lambda b, h, i, ks, cfg: (tc // s_lat + b, 3 * h_a + 2 * h_b + h)),
                      pl.BlockSpec((None, past, hd), lambda b, h, i, ks, cfg: (b, 0, h)),
                      pl.BlockSpec((None, past, hd), lambda b, h, i, ks, cfg: (b, 0, h)),
                      pl.BlockSpec((None, None, tqn, kw), lambda b, h, i, ks, cfg: (h, cfg[i], 0, 0))],
            out_specs=pl.BlockSpec((tqn, hd), lambda b, h, i, ks, cfg: (tc // tqn + b * nqn + i, h_a + h)),
        ),
        out_shape=o_shape,
        input_output_aliases={2: 0},
        compiler_params=_cp("parallel", "parallel", "arbitrary"),
        name="attn_nbr_lat",
    )(jnp.asarray(ks_np), jnp.asarray(cfg_np), o, p, p, p, ck_b, cv_b, bias)
    return o


def _attn_mla(q, kv, krd, krd_cache, h_c, dims, tables):
    bc, seq, bs, s_lat, past = dims
    t = q.shape[0]
    tc = bc * seq
    cos, sup, sdn = tables
    hp = h_c // 2
    w2 = 2 * NOPE_DIM
    o_shape = jax.ShapeDtypeStruct((t, h_c * V_DIM), bf16)

    npair = _pick(hp, CTX_HEADS // 2)
    gp = hp // npair
    wn, wr = npair * w2, npair * LANES
    o = pl.pallas_call(
        functools.partial(_mla_ctx_kernel, npair=npair),
        grid=(bc, gp),
        in_specs=[pl.BlockSpec((seq, wn), lambda b, h: (b, h)),
                  pl.BlockSpec((seq, wr), lambda b, h: (b, 2 * gp + h)),
                  pl.BlockSpec((seq, wn), lambda b, h: (b, h)),
                  pl.BlockSpec((seq, wn), lambda b, h: (b, gp + h)),
                  pl.BlockSpec((seq, LANES), lambda b, h: (b, 0))],
        out_specs=pl.BlockSpec((seq, wn), lambda b, h: (b, h)),
        out_shape=o_shape,
        compiler_params=_cp("parallel", "parallel"),
        name="attn_mla_ctx",
    )(q, q, kv, kv, krd)

    tq = _pick(s_lat, Q_TILE)
    nq = s_lat // tq
    full = lambda b, h, i: (0, 0)
    o = pl.pallas_call(
        lambda o_in, *refs: _mla_lat_kernel(*refs, tq=tq, s_self=s_lat, s_cache=past, nq=nq),
        grid=(bs, hp, nq),
        in_specs=[pl.BlockSpec(memory_space=pl.ANY),
                  pl.BlockSpec((tq, w2), lambda b, h, i: (tc // tq + b * nq + i, h)),
                  pl.BlockSpec((tq, LANES), lambda b, h, i: (tc // tq + b * nq + i, 2 * hp + h)),
                  pl.BlockSpec((tq, w2), lambda b, h, i: (tc // tq + b * nq + jnp.minimum(i + 1, nq - 1), h)),
                  pl.BlockSpec((tq, LANES),
                               lambda b, h, i: (tc // tq + b * nq + jnp.minimum(i + 1, nq - 1), 2 * hp + h)),
                  pl.BlockSpec((s_lat, w2), lambda b, h, i: (tc // s_lat + b, h)),
                  pl.BlockSpec((s_lat, w2), lambda b, h, i: (tc // s_lat + b, hp + h)),
                  pl.BlockSpec((s_lat, LANES), lambda b, h, i: (tc // s_lat + b, 0)),
                  pl.BlockSpec((past, w2), lambda b, h, i: (t // past + b, h)),
                  pl.BlockSpec((past, w2), lambda b, h, i: (t // past + b, hp + h)),
                  pl.BlockSpec((None, past, LANES), lambda b, h, i: (b, 0, 0)),
                  pl.BlockSpec((s_lat, LANES), full), pl.BlockSpec((s_lat, LANES), full),
                  pl.BlockSpec((s_lat, LANES), full)],
        out_specs=pl.BlockSpec((tq, w2), lambda b, h, i: (tc // tq + b * nq + i, h)),
        out_shape=o_shape,
        scratch_shapes=[pltpu.VMEM((2, s_lat + past, w2), bf16), pltpu.VMEM((2, s_lat + past, 2 * V_DIM), bf16),
                        pltpu.VMEM((2, tq, s_lat + past), f32), pltpu.VMEM((2, tq, s_lat + past), f32),
                        pltpu.VMEM((2, tq, 1), f32), pltpu.VMEM((2, tq, 1), f32)],
        input_output_aliases={0: 0},
        compiler_params=_cp("parallel", "parallel", "arbitrary"),
        name="attn_mla_lat",
    )(o, q, q, q, q, kv, kv, krd, kv, kv, krd_cache, cos, sup, sdn)
    return o


def _router_kernel(h_ref, w_ref, b_ref, eid_ref, wgt_ref, cnt_ref):
    logits = jnp.dot(h_ref[...], w_ref[...], preferred_element_type=f32) + b_ref[...]
    lane = lax.broadcasted_iota(jnp.int32, logits.shape, 1)
    big = jnp.int32(LANES)
    neg = -jnp.inf

    def top(vals):
        v = jnp.max(vals, axis=-1, keepdims=True)
        return v, jnp.min(jnp.where(vals == v, lane, big), axis=-1, keepdims=True)

    gl = jnp.where(lane < N_GROUPS, logits, neg)
    gmax, g_idx = top(gl)
    p_group = 1.0 / jnp.sum(jnp.exp(gl - gmax), axis=-1, keepdims=True)
    lo = N_GROUPS + g_idx * EXPERTS_PER_GROUP
    el = jnp.where((lane >= lo) & (lane < lo + EXPERTS_PER_GROUP), logits, neg)
    v1, i1 = top(el)
    v2, i2 = top(jnp.where(lane == i1, neg, el))
    e = jnp.exp(v2 - v1)
    w1 = p_group / (1.0 + e)
    eid_ref[...] = jnp.where(lane == 0, i1 - N_GROUPS, jnp.where(lane == 1, i2 - N_GROUPS, 0))
    wgt_ref[...] = jnp.where(lane == 0, w1, jnp.where(lane == 1, w1 * e, 0.0))
    cnt_ref[...] = jnp.sum(((lane == i1) | (lane == i2)).astype(jnp.int32), axis=0, keepdims=True)


def _router(h, w_r, b_r):
    t, d = h.shape
    tm = _pick(t, ROW_TILE)
    return pl.pallas_call(
        _router_kernel,
        grid=(t // tm,),
        in_specs=[pl.BlockSpec((tm, d), lambda i: (i, 0)),
                  pl.BlockSpec((d, LANES), lambda i: (0, 0)),
                  pl.BlockSpec((1, LANES), lambda i: (0, 0))],
        out_specs=[pl.BlockSpec((tm, LANES), lambda i: (i, 0)), pl.BlockSpec((tm, LANES), lambda i: (i, 0)),
                   pl.BlockSpec((None, 1, LANES), lambda i: (i, 0, 0))],
        out_shape=[jax.ShapeDtypeStruct((t, LANES), jnp.int32), jax.ShapeDtypeStruct((t, LANES), f32),
                   jax.ShapeDtypeStruct((t // tm, 1, LANES), jnp.int32)],
        compiler_params=_cp("parallel"),
        name="router",
    )(h, w_r, b_r)


def _moe_kernel(te_ref, nu_ref, tok_ref, dst_ref, hp_ref, rw_ref, w1_ref, w3_ref, w2_ref, y_ref,
                xg0, xg1, yb0, yb1, gsem, ssem, *, tm):
    del te_ref
    i = pl.program_id(0)
    n_used = nu_ref[0]
    xg, yb = (xg0, xg1), (yb0, yb1)

    def gather_copy(row, r, slot):
        return pltpu.make_async_copy(hp_ref.at[pl.ds(row, 1)], xg[slot].at[pl.ds(r, 1)], gsem.at[slot])

    def scatter_copy(row, r, slot):
        return pltpu.make_async_copy(yb[slot].at[pl.ds(r, 1)], y_ref.at[pl.ds(row, 1)], ssem.at[slot])

    def gather_wait(slot):
        pltpu.make_async_copy(hp_ref.at[pl.ds(0, tm)], xg[slot], gsem.at[slot]).wait()

    def scatter_wait(slot):
        pltpu.make_async_copy(yb[slot], y_ref.at[pl.ds(0, tm)], ssem.at[slot]).wait()

    def compute(slot):
        u = xg[slot][...]
        half = u.shape[1]
        x_lo = lax.bitcast_convert_type(u << 16, f32).astype(bf16)
        x_hi = lax.bitcast_convert_type(u & jnp.uint32(0xFFFF0000), f32).astype(bf16)
        a = (jnp.dot(x_lo, w1_ref[0:half, :], preferred_element_type=f32)
             + jnp.dot(x_hi, w1_ref[half:2 * half, :], preferred_element_type=f32))
        b = (jnp.dot(x_lo, w3_ref[0:half, :], preferred_element_type=f32)
             + jnp.dot(x_hi, w3_ref[half:2 * half, :], preferred_element_type=f32))
        hid = (jax.nn.silu(a) * b * rw_ref[...]).astype(bf16)
        yb[slot][...] = jnp.dot(hid, w2_ref[...], preferred_element_type=f32)

    @pl.when(i == 0)
    def _():
        yb1[...] = jnp.zeros_like(yb1)

        def prime(r, carry):
            gather_copy(tok_ref[r], r, 0).start()
            return carry

        lax.fori_loop(0, tm, prime, 0)

    for parity in (0, 1):
        cur, nxt = parity, 1 - parity

        @pl.when((i % 2 == parity) & (i >= 1) & (i <= n_used))
        def _():
            scatter_wait(cur)

        @pl.when((i % 2 == parity) & (i < n_used))
        def _():
            gather_wait(cur)
            for r in range(tm):
                gather_copy(tok_ref[(i + 1) * tm + r], r, nxt).start()
            for r in range(tm):
                scatter_copy(dst_ref[i * tm + r], r, nxt).start()

        @pl.when((i % 2 == parity) & (i + 1 <= n_used))
        def _():
            compute(cur)

        @pl.when((i % 2 == parity) & (i == n_used))
        def _():
            gather_wait(cur)

            def flush(r, carry):
                scatter_copy(dst_ref[i * tm + r], r, nxt).start()
                return carry

            lax.fori_loop(0, tm, flush, 0)
            scatter_wait(nxt)


def _moe(hp, eid, wgt, counts, w1, w3, w2):
    t, half = hp.shape
    d = 2 * half
    n_exp, _, f = w1.shape
    tm = MOE_TILE
    n_assign = 2 * t
    n_tiles = n_assign // tm + n_exp + 1
    n_rows = n_tiles * tm

    e_flat = eid.T.reshape(-1)
    w_flat = wgt.T.reshape(-1)
    order = jnp.argsort(e_flat, stable=True).astype(jnp.int32)
    padded = ((counts + tm - 1) // tm) * tm
    pad_end = jnp.cumsum(padded)
    pad_start = pad_end - padded
    src_start = jnp.cumsum(counts) - counts
    n_used = (pad_end[-1] // tm).astype(jnp.int32).reshape(1)
    tile_start = jnp.arange(n_tiles, dtype=jnp.int32) * tm
    tile_e = jnp.minimum(jnp.sum((pad_end[None, :] <= tile_start[:, None]).astype(jnp.int32), axis=1), n_exp - 1)
    rows = jnp.arange(n_rows, dtype=jnp.int32)
    row_e = jnp.broadcast_to(tile_e[:, None], (n_tiles, tm)).reshape(-1)
    j = rows - pad_start[row_e]
    valid = (j < counts[row_e]) & (rows < pad_end[-1])
    a = order[jnp.clip(src_start[row_e] + j, 0, n_assign - 1)]
    spare = n_assign + rows % tm
    row_tok = jnp.where(valid, a % t, 0).astype(jnp.int32)
    row_dst = jnp.concatenate([spare[:tm], jnp.where(valid, a, spare)]).astype(jnp.int32)
    row_w = jnp.where(valid, w_flat[a], 0.0).astype(f32).reshape(n_rows, 1)

    return pl.pallas_call(
        functools.partial(_moe_kernel, tm=tm),
        grid_spec=pltpu.PrefetchScalarGridSpec(
            num_scalar_prefetch=4,
            grid=(n_tiles,),
            in_specs=[pl.BlockSpec(memory_space=pl.ANY),
                      pl.BlockSpec((tm, 1), lambda i, te, nu, tok, dst: (i, 0)),
                      pl.BlockSpec((None, d, f), lambda i, te, nu, tok, dst: (te[i], 0, 0)),
                      pl.BlockSpec((None, d, f), lambda i, te, nu, tok, dst: (te[i], 0, 0)),
                      pl.BlockSpec((None, f, d), lambda i, te, nu, tok, dst: (te[i], 0, 0))],
            out_specs=pl.BlockSpec(memory_space=pl.ANY),
            scratch_shapes=[pltpu.VMEM((tm, half), jnp.uint32), pltpu.VMEM((tm, half), jnp.uint32),
                            pltpu.VMEM((tm, d), f32), pltpu.VMEM((tm, d), f32),
                            pltpu.SemaphoreType.DMA((2,)), pltpu.SemaphoreType.DMA((2,))],
        ),
        out_shape=jax.ShapeDtypeStruct((n_assign + tm, d), f32),
        compiler_params=_cp("arbitrary"),
        name="moe",
    )(tile_e, n_used, row_tok, row_dst, hp, row_w, w1, w3, w2)


def kernel(x_prompt, x_sample, c, c_ctx, cache_a_k, cache_a_v, cache_b_k, cache_b_v, cache_c_kv, cache_c_kr,
           g_norm1, g_norm2, g_final, w_ada, b_ada, w_in_ab, w_out_ab, lam_q1, lam_k1, lam_q2, lam_k2,
           g_sub_a, rpb_b, w_down_c, g_q_c, g_kv_c, w_uq_c, w_uk_c, w_uv_c, w_out_c,
           w_group_router, b_group_router, w_expert_router, b_expert_router, w1_moe, w3_moe, w2_moe):
    bc, seq, d = x_prompt.shape
    bs, s_lat, _ = x_sample.shape
    past = cache_a_k.shape[2]
    depth = w_ada.shape[0]
    tc, ts = bc * seq, bs * s_lat
    t = tc + ts
    h_a, h_b = cache_a_k.shape[3], cache_b_k.shape[3]
    ab = (h_a + h_b) * HEAD_DIM
    dims = (bc, seq, bs, s_lat, past)
    assert tc % s_lat == 0 and t % past == 0 and s_lat % (2 * ROW_TILE) == 0

    x = jnp.concatenate([x_prompt.reshape(tc, d), x_sample.reshape(ts, d)], axis=0)
    cond = jnp.concatenate([c_ctx[None], c, jnp.zeros((COND_ROWS_PAD - 1 - bs, d), f32)], axis=0)
    mod = _ada_mod(cond, w_ada, b_ada)
    tables = _rope_tables(s_lat)

    n_exp = N_GROUPS * EXPERTS_PER_GROUP
    sak, sav, sbk, sbv, sckv, sckr = [], [], [], [], [], []
    moe_pending = None
    for l in range(depth):
        mod4 = mod[l].reshape(COND_ROWS_PAD, 6, 1, d)
        if moe_pending is None:
            h = _norm(x, g_norm1[l], mod4, tc, s_lat, sel=0)[0]
        else:
            x, h = _norm(x, g_norm1[l], mod4, tc, s_lat, sel=0, moe=moe_pending)
        if l % 2 == 0:
            e = l // 2
            lam_init = 0.8 - 0.6 * math.exp(-0.3 * l)
            lam = (jnp.exp(jnp.sum((lam_q1[e] * lam_k1[e]).astype(f32)))
                   - jnp.exp(jnp.sum((lam_q2[e] * lam_k2[e]).astype(f32))) + lam_init)
            p = _matmul(h, w_in_ab[e].astype(bf16), f32)
            wa = h_a * HEAD_DIM
            sak.append(p[:tc, wa:2 * wa].reshape(bc, seq, h_a, HEAD_DIM))
            sav.append(p[:tc, 2 * wa:3 * wa].reshape(bc, seq, h_a, HEAD_DIM))
            wb = h_b * HEAD_DIM
            sbk.append(p[:tc, 3 * wa + wb:3 * wa + 2 * wb].reshape(bc, seq, h_b, HEAD_DIM))
            sbv.append(p[:tc, 3 * wa + 2 * wb:].reshape(bc, seq, h_b, HEAD_DIM))
            o = _attn_even(p, ab, lam, lam_init, g_sub_a[e], rpb_b[e],
                           cache_a_k[:, e].reshape(bs, past, wa), cache_a_v[:, e].reshape(bs, past, wa),
                           cache_b_k[:, e].reshape(bs, past, wb), cache_b_v[:, e].reshape(bs, past, wb),
                           dims, tables)
            w_out = w_out_ab[e].astype(bf16)
        else:
            oi = l // 2
            q_lora, kv_lora = g_q_c.shape[1], g_kv_c.shape[1]
            h_c = w_uk_c.shape[2] // NOPE_DIM
            wd = w_down_c[oi]
            wd = jnp.concatenate([wd, wd[:, q_lora + kv_lora:]], axis=1).astype(bf16)
            cq, ckv, krd = _mla_down(h, wd, g_q_c[oi], g_kv_c[oi])
            sckv.append(ckv[:tc].reshape(bc, seq, kv_lora))
            sckr.append(krd[:tc, :ROPE_DIM].reshape(bc, seq, ROPE_DIM))
            wq = w_uq_c[oi].reshape(q_lora, h_c, NOPE_DIM + ROPE_DIM)
            wq = jnp.concatenate([wq[:, :, :NOPE_DIM].reshape(q_lora, -1), wq[:, :, NOPE_DIM:].reshape(q_lora, -1)],
                                 axis=1).astype(bf16)
            q = _matmul(cq, wq, f32)
            ckv_all = jnp.concatenate([ckv, cache_c_kv[:, oi].reshape(bs * past, kv_lora)], axis=0).astype(bf16)
            wkv = jnp.concatenate([w_uk_c[oi], w_uv_c[oi]], axis=1).astype(bf16)
            kv = _matmul(ckv_all, wkv, bf16)
            krc = cache_c_kr[:, oi]
            o = _attn_mla(q, kv, krd, jnp.concatenate([krc, krc], axis=-1), h_c, dims, tables)
            w_out = w_out_c[oi].astype(bf16)
        x = _matmul(o, w_out, f32, residual=(x, mod4, 2, tc, s_lat))
        h2, hp = _norm(x, g_norm2[l], mod4, tc, s_lat, sel=1, packed=True)
        w_r = jnp.concatenate([w_group_router[l], w_expert_router[l]], axis=1)
        w_r = jnp.pad(w_r, ((0, 0), (0, LANES - w_r.shape[1]))).astype(bf16)
        b_r = jnp.pad(jnp.concatenate([b_group_router[l], b_expert_router[l]]), (0, LANES - N_GROUPS - n_exp))
        eid, wgt, cnt = _router(h2, w_r, b_r.reshape(1, LANES).astype(f32))
        counts = jnp.sum(cnt, axis=(0, 1))[N_GROUPS:N_GROUPS + n_exp]
        f = w1_moe.shape[-1]
        y = _moe(hp, eid[:, :2], wgt[:, :2], counts,
                 w1_moe[l].reshape(n_exp, d, f).astype(bf16), w3_moe[l].reshape(n_exp, d, f).astype(bf16),
                 w2_moe[l].reshape(n_exp, f, d).astype(bf16))
        moe_pending = (y, mod4, 5)
    out = _norm(x, g_final, None, tc, s_lat, moe=moe_pending, final=True)[0]
    y_prompt = out[:tc].reshape(bc, seq, d)
    y_sample = out[tc:].reshape(bs, s_lat, d)
    return (y_prompt, y_sample, jnp.stack(sak, axis=1), jnp.stack(sav, axis=1), jnp.stack(sbk, axis=1),
            jnp.stack(sbv, axis=1), jnp.stack(sckv, axis=1), jnp.stack(sckr, axis=1))
```

```python
import functools
import math

import numpy as np
import jax
import jax.numpy as jnp
from jax import lax
from jax.experimental import pallas as pl
from jax.experimental.pallas import tpu as pltpu

GRID_W = 64
HEAD_DIM = 128
WIN_ROWS = 8
WIN_COLS = 16
NOPE_DIM = 128
ROPE_DIM = 64
V_DIM = 128
N_GROUPS = 4
EXPERTS_PER_GROUP = 8
ROPE_THETA = 10000.0
NORM_EPS = 1e-6
NEG_INF = -1e30
LOG2E = math.log2(math.e)

LANES = 128
COND_ROWS_PAD = 16
VMEM_LIMIT = 56 * 1024 * 1024
ROW_TILE = 512
COL_TILE = 512
NORM_TILE = 256
MOE_TILE = 256
Q_TILE = 512
NBR_ROWS = 4
CTX_HEADS = 8
NBR_HEADS = 2

f32 = jnp.float32
bf16 = jnp.bfloat16


def _cp(*sem):
    return pltpu.CompilerParams(dimension_semantics=sem, vmem_limit_bytes=VMEM_LIMIT)


def _pick(n, pref):
    if n <= pref:
        return n
    t = pref
    while n % t:
        t //= 2
    return t


def _cond_row(r0, tc, s_lat):
    return jnp.where(r0 < tc, 0, 1 + (r0 - tc) // s_lat)


def _dot_nt(a, b):
    return lax.dot_general(a, b, (((1,), (1,)), ((), ())), preferred_element_type=f32)


def _ada_kernel(c_ref, w_ref, b_ref, o_ref):
    s = jax.nn.silu(c_ref[...]).astype(bf16)
    o_ref[...] = jnp.dot(s, w_ref[...].astype(bf16), preferred_element_type=f32) + b_ref[...]


def _ada_mod(cond, w_ada, b_ada):
    depth, d, n = w_ada.shape
    r = cond.shape[0]
    tn = _pick(n, COL_TILE)
    return pl.pallas_call(
        _ada_kernel,
        grid=(depth, n // tn),
        in_specs=[pl.BlockSpec((r, d), lambda l, j: (0, 0)),
                  pl.BlockSpec((None, d, tn), lambda l, j: (l, 0, j)),
                  pl.BlockSpec((None, 1, tn), lambda l, j: (l, 0, j))],
        out_specs=pl.BlockSpec((None, r, tn), lambda l, j: (l, 0, j)),
        out_shape=jax.ShapeDtypeStruct((depth, r, n), f32),
        compiler_params=_cp("parallel", "parallel"),
        name="ada_mod",
    )(cond, w_ada, b_ada.reshape(depth, 1, n))


def _norm_kernel(*refs, combine, final, packed):
    it = iter(refs)
    x_ref = next(it)
    if combine:
        y0_ref, y1_ref, gate_ref = next(it), next(it), next(it)
    g_ref = next(it)
    if not final:
        shift_ref, scale_ref = next(it), next(it)
    x = x_ref[...]
    if combine:
        x = x + gate_ref[...] * (y0_ref[...] + y1_ref[...])
        if not final:
            next(it)[...] = x
    y = x * lax.rsqrt(jnp.mean(x * x, axis=-1, keepdims=True) + NORM_EPS) * g_ref[...]
    if final:
        next(it)[...] = y
        return
    h = (y * (1.0 + scale_ref[...]) + shift_ref[...]).astype(bf16)
    next(it)[...] = h
    if packed:
        bits = lax.bitcast_convert_type(h.astype(f32), jnp.uint32)
        half = bits.shape[1] // 2
        next(it)[...] = (bits[:, half:] & jnp.uint32(0xFFFF0000)) | (bits[:, :half] >> 16)


def _norm(x, g, mod4, tc, s_lat, *, sel=None, moe=None, final=False, packed=False):
    t, d = x.shape
    tm = NORM_TILE
    nt = t // tm
    combine = moe is not None
    row = lambda i: (i, 0)
    modspec = lambda j: pl.BlockSpec((None, None, 1, d), lambda i: (_cond_row(i * tm, tc, s_lat), j, 0, 0))
    args, specs = [x], [pl.BlockSpec((tm, d), row)]
    if combine:
        y, mod4_moe, gate_j = moe
        args += [y, y, mod4_moe]
        specs += [pl.BlockSpec((tm, d), row), pl.BlockSpec((tm, d), lambda i: (i + nt, 0)), modspec(gate_j)]
    args.append(g.reshape(1, d))
    specs.append(pl.BlockSpec((1, d), lambda i: (0, 0)))
    if not final:
        args += [mod4, mod4]
        specs += [modspec(3 * sel), modspec(3 * sel + 1)]
    shapes, ospecs = [], []
    if combine and not final:
        shapes.append(jax.ShapeDtypeStruct((t, d), f32))
        ospecs.append(pl.BlockSpec((tm, d), row))
    if final:
        shapes.append(jax.ShapeDtypeStruct((t, d), f32))
        ospecs.append(pl.BlockSpec((tm, d), row))
    else:
        shapes.append(jax.ShapeDtypeStruct((t, d), bf16))
        ospecs.append(pl.BlockSpec((tm, d), row))
        if packed:
            shapes.append(jax.ShapeDtypeStruct((t, d // 2), jnp.uint32))
            ospecs.append(pl.BlockSpec((tm, d // 2), row))
    return pl.pallas_call(
        functools.partial(_norm_kernel, combine=combine, final=final, packed=packed),
        grid=(nt,),
        in_specs=specs,
        out_specs=ospecs,
        out_shape=shapes,
        input_output_aliases={0: 0} if (combine and not final) else {},
        compiler_params=_cp("parallel"),
        name="norm",
    )(*args)


def _mm_kernel(*refs, residual):
    if residual:
        a_ref, b_ref, x_ref, gate_ref, o_ref = refs
    else:
        a_ref, b_ref, o_ref = refs
    acc = jnp.dot(a_ref[...], b_ref[...], preferred_element_type=f32)
    if residual:
        acc = x_ref[...] + gate_ref[...] * acc
    o_ref[...] = acc.astype(o_ref.dtype)


def _matmul(a, b, out_dtype, *, residual=None):
    m, k = a.shape
    n = b.shape[1]
    tm = _pick(m, 2 * ROW_TILE if (k >= 4 * ROW_TILE or k <= ROW_TILE) else ROW_TILE)
    tn = _pick(n, COL_TILE * max(1, min(4, 4 * ROW_TILE // k)))
    args = [a, b]
    specs = [pl.BlockSpec((tm, k), lambda i, j: (i, 0)), pl.BlockSpec((k, tn), lambda i, j: (0, j))]
    aliases = {}
    if residual is not None:
        x, mod4, gate_j, tc, s_lat = residual
        args += [x, mod4]
        specs += [pl.BlockSpec((tm, tn), lambda i, j: (i, j)),
                  pl.BlockSpec((None, None, 1, tn), lambda i, j: (_cond_row(i * tm, tc, s_lat), gate_j, 0, j))]
        aliases = {2: 0}
    return pl.pallas_call(
        functools.partial(_mm_kernel, residual=residual is not None),
        grid=(m // tm, n // tn),
        in_specs=specs,
        out_specs=pl.BlockSpec((tm, tn), lambda i, j: (i, j)),
        out_shape=jax.ShapeDtypeStruct((m, n), out_dtype),
        input_output_aliases=aliases,
        compiler_params=_cp("parallel", "parallel"),
        name="matmul",
    )(*args)


def _down_kernel(a_ref, w_ref, gq_ref, gkv_ref, cq_ref, ckv_ref, kr_ref, *, q_lora, kv_lora):
    acc = jnp.dot(a_ref[...], w_ref[...], preferred_element_type=f32)
    cq = acc[:, :q_lora]
    cq = cq * lax.rsqrt(jnp.mean(cq * cq, axis=-1, keepdims=True) + NORM_EPS) * gq_ref[...]
    cq_ref[...] = cq.astype(bf16)
    ckv = acc[:, q_lora:q_lora + kv_lora]
    ckv_ref[...] = ckv * lax.rsqrt(jnp.mean(ckv * ckv, axis=-1, keepdims=True) + NORM_EPS) * gkv_ref[...]
    kr_ref[...] = acc[:, q_lora + kv_lora:]


def _mla_down(h, w_down_dup, g_q, g_kv):
    t, d = h.shape
    q_lora, kv_lora = g_q.shape[0], g_kv.shape[0]
    n = w_down_dup.shape[1]
    tm = _pick(t, ROW_TILE)
    return pl.pallas_call(
        functools.partial(_down_kernel, q_lora=q_lora, kv_lora=kv_lora),
        grid=(t // tm,),
        in_specs=[pl.BlockSpec((tm, d), lambda i: (i, 0)),
                  pl.BlockSpec((d, n), lambda i: (0, 0)),
                  pl.BlockSpec((1, q_lora), lambda i: (0, 0)),
                  pl.BlockSpec((1, kv_lora), lambda i: (0, 0))],
        out_specs=[pl.BlockSpec((tm, q_lora), lambda i: (i, 0)),
                   pl.BlockSpec((tm, kv_lora), lambda i: (i, 0)),
                   pl.BlockSpec((tm, LANES), lambda i: (i, 0))],
        out_shape=[jax.ShapeDtypeStruct((t, q_lora), bf16),
                   jax.ShapeDtypeStruct((t, kv_lora), f32),
                   jax.ShapeDtypeStruct((t, LANES), f32)],
        compiler_params=_cp("parallel"),
        name="mla_down",
    )(h, w_down_dup, g_q.reshape(1, -1), g_kv.reshape(1, -1))


def _rope_tables(s_lat):
    half = ROPE_DIM // 2
    quarter = half // 2
    tpos = jnp.arange(s_lat)
    inv = ROPE_THETA ** (-jnp.arange(0, half, 2, dtype=f32) / half)
    ang_r = (tpos // GRID_W).astype(f32)[:, None] * inv
    ang_c = (tpos % GRID_W).astype(f32)[:, None] * inv
    ang = jnp.concatenate([ang_r, ang_r, ang_c, ang_c], axis=-1)
    ang = jnp.tile(ang, (1, LANES // ROPE_DIM))
    first = (jnp.arange(LANES) % half) < quarter
    sin = jnp.sin(ang)
    return jnp.cos(ang), jnp.where(first, -sin, 0.0), jnp.where(first, 0.0, sin)


def _rope(x, cos, sin_up, sin_dn):
    quarter = ROPE_DIM // 4
    return x * cos + pltpu.roll(x, LANES - quarter, 1) * sin_up + pltpu.roll(x, quarter, 1) * sin_dn


def _softmax_parts(s):
    m = jnp.max(s, axis=-1, keepdims=True)
    e = jnp.exp2(s - m)
    return e, jnp.sum(e, axis=-1, keepdims=True)


def _diff_logits(q, kk):
    dqk = HEAD_DIM // 2
    q = q * (dqk ** -0.5 * LOG2E)
    lane = lax.broadcasted_iota(jnp.int32, q.shape, 1)
    return (_dot_nt(jnp.where(lane < dqk, q, 0.0).astype(bf16), kk),
            _dot_nt(jnp.where(lane >= dqk, q, 0.0).astype(bf16), kk))


def _diff_finish(s1, m1, s2, m2, vv, lam, g, lam_init):
    e1, e2 = jnp.exp2(s1 - m1), jnp.exp2(s2 - m2)
    l1, l2 = jnp.sum(e1, axis=-1, keepdims=True), jnp.sum(e2, axis=-1, keepdims=True)
    a = e1 * (1.0 / l1) - e2 * (lam / l2)
    o = jnp.dot(a.astype(bf16), vv, preferred_element_type=f32)
    o = o * lax.rsqrt(jnp.mean(o * o, axis=-1, keepdims=True) + NORM_EPS) * g
    return (o * (1.0 - lam_init)).astype(bf16)


def _diff_finish_aug(s1, m1, s2, m2, vaug, lam, g, lam_init):
    o1 = jnp.dot(jnp.exp2(s1 - m1).astype(bf16), vaug, preferred_element_type=f32)
    o2 = jnp.dot(jnp.exp2(s2 - m2).astype(bf16), vaug, preferred_element_type=f32)
    hd = HEAD_DIM
    o = o1[:, :hd] * (1.0 / o1[:, hd:hd + 1]) - o2[:, :hd] * (lam / o2[:, hd:hd + 1])
    o = o * lax.rsqrt(jnp.mean(o * o, axis=-1, keepdims=True) + NORM_EPS) * g
    return (o * (1.0 - lam_init)).astype(bf16)


def _diff_ctx_kernel(lam_ref, q_ref, k_ref, v_ref, g_ref, o_ref, *, nh, lam_init):
    for hh in range(nh):
        cols = slice(hh * HEAD_DIM, (hh + 1) * HEAD_DIM)
        s1, s2 = _diff_logits(q_ref[:, cols], k_ref[:, cols].astype(bf16))
        o_ref[:, cols] = _diff_finish(s1, jnp.max(s1, axis=-1, keepdims=True), s2, jnp.max(s2, axis=-1, keepdims=True),
                                      v_ref[:, cols].astype(bf16), lam_ref[0], g_ref[...], lam_init)


def _diff_lat_kernel(lam_ref, q_ref, q_nx_ref, k_ref, v_ref, kc_ref, vc_ref, cos_ref, sup_ref, sdn_ref, g_ref, o_ref,
                     kall, vall, s0, s1, m0, m1, *, tq, s_self, s_cache, nq, lam_init):
    qi = pl.program_id(2)
    bufs = ((s0, m0), (s1, m1))

    def logits(q_r, tile, s_buf, m_buf):
        rows = pl.ds(pl.multiple_of(tile * tq, tq), tq)
        q = _rope(q_r[...], cos_ref[rows, :], sup_ref[rows, :], sdn_ref[rows, :])
        for c, s in enumerate(_diff_logits(q, kall[...])):
            s_buf[c] = s
            m_buf[c] = jnp.max(s, axis=-1, keepdims=True)

    @pl.when(qi == 0)
    def _():
        kall[0:s_self, :] = _rope(k_ref[...], cos_ref[...], sup_ref[...], sdn_ref[...]).astype(bf16)
        kall[s_self:s_self + s_cache, :] = kc_ref[...].astype(bf16)
        vall[0:s_self, 0:HEAD_DIM] = v_ref[...].astype(bf16)
        vall[s_self:s_self + s_cache, 0:HEAD_DIM] = vc_ref[...].astype(bf16)
        vall[:, HEAD_DIM:2 * HEAD_DIM] = jnp.ones((s_self + s_cache, HEAD_DIM), bf16)
        logits(q_ref, 0, s0, m0)

    for parity in (0, 1):
        (s_cur, m_cur), (s_nxt, m_nxt) = bufs[parity], bufs[1 - parity]

        @pl.when(qi % 2 == parity)
        def _():
            logits(q_nx_ref, jnp.minimum(qi + 1, nq - 1), s_nxt, m_nxt)
            o_ref[...] = _diff_finish_aug(s_cur[0], m_cur[0], s_cur[1], m_cur[1], vall[...], lam_ref[0],
                                          g_ref[...], lam_init)


def _soft_ctx_kernel(q_ref, k_ref, v_ref, o_ref, *, nh):
    for hh in range(nh):
        cols = slice(hh * HEAD_DIM, (hh + 1) * HEAD_DIM)
        q = (q_ref[:, cols] * (HEAD_DIM ** -0.5 * LOG2E)).astype(bf16)
        e, l = _softmax_parts(_dot_nt(q, k_ref[:, cols].astype(bf16)))
        o = jnp.dot(e.astype(bf16), v_ref[:, cols].astype(bf16), preferred_element_type=f32)
        o_ref[:, cols] = (o * (1.0 / l)).astype(bf16)


def _nbr_kernel(ks_ref, cfg_ref, q_ref, k_ref, v_ref, kc_ref, vc_ref, bias_ref, o_ref, *, kw, nh):
    del cfg_ref
    t = pl.program_id(2)
    rows = pl.ds(pl.multiple_of(ks_ref[t] * GRID_W, GRID_W), kw)
    for hh in range(nh):
        c = slice(hh * HEAD_DIM, (hh + 1) * HEAD_DIM)
        q = (q_ref[:, c] * (HEAD_DIM ** -0.5 * LOG2E)).astype(bf16)
        s_loc = _dot_nt(q, k_ref[rows, c].astype(bf16)) + bias_ref[hh]
        s_ctx = _dot_nt(q, kc_ref[:, c].astype(bf16))
        m = jnp.maximum(jnp.max(s_loc, axis=-1, keepdims=True), jnp.max(s_ctx, axis=-1, keepdims=True))
        e_loc, e_ctx = jnp.exp2(s_loc - m), jnp.exp2(s_ctx - m)
        l = jnp.sum(e_loc, axis=-1, keepdims=True) + jnp.sum(e_ctx, axis=-1, keepdims=True)
        o = (jnp.dot(e_loc.astype(bf16), v_ref[rows, c].astype(bf16), preferred_element_type=f32)
             + jnp.dot(e_ctx.astype(bf16), vc_ref[:, c].astype(bf16), preferred_element_type=f32))
        o_ref[:, c] = (o * (1.0 / l)).astype(bf16)


def _mla_head(qn, qr_masked, kcat, vv):
    qc = (jnp.concatenate([qn, qr_masked], axis=-1) * ((NOPE_DIM + ROPE_DIM) ** -0.5 * LOG2E)).astype(bf16)
    e, l = _softmax_parts(_dot_nt(qc, kcat))
    o = jnp.dot(e.astype(bf16), vv, preferred_element_type=f32)
    return (o * (1.0 / l)).astype(bf16)


def _mla_ctx_kernel(qn_ref, qr_ref, kn_ref, v_ref, kr_ref, o_ref, *, npair):
    hd = NOPE_DIM
    kr = kr_ref[...].astype(bf16)
    lane = lax.broadcasted_iota(jnp.int32, (qr_ref.shape[0], LANES), 1)
    for pp in range(npair):
        qr = qr_ref[:, pp * LANES:(pp + 1) * LANES]
        for hh in range(2):
            c = slice((2 * pp + hh) * hd, (2 * pp + hh + 1) * hd)
            mask = (lane < ROPE_DIM) if hh == 0 else (lane >= ROPE_DIM)
            kcat = jnp.concatenate([kn_ref[:, c], kr], axis=-1)
            o_ref[:, c] = _mla_head(qn_ref[:, c], jnp.where(mask, qr, 0.0), kcat, v_ref[:, c])


def _mla_lat_kernel(qn_ref, qr_ref, qn_nx_ref, qr_nx_ref, kn_ref, v_ref, kr_ref, knc_ref, vc_ref, krc_ref,
                    cos_ref, sup_ref, sdn_ref, o_ref, kcat, vaug, s0, s1, m0, m1, *, tq, s_self, s_cache, nq):
    qi = pl.program_id(2)
    hd = NOPE_DIM
    s_all = s_self + s_cache
    bufs = ((s0, m0), (s1, m1))

    def logits(qn_r, qr_r, tile, s_buf, m_buf):
        rows = pl.ds(pl.multiple_of(tile * tq, tq), tq)
        qr = _rope(qr_r[...], cos_ref[rows, :], sup_ref[rows, :], sdn_ref[rows, :])
        lane = lax.broadcasted_iota(jnp.int32, qr.shape, 1)
        for hh in range(2):
            mask = (lane < ROPE_DIM) if hh == 0 else (lane >= ROPE_DIM)
            qc = jnp.concatenate([qn_r[:, hh * hd:(hh + 1) * hd], jnp.where(mask, qr, 0.0)], axis=-1)
            s = _dot_nt((qc * ((NOPE_DIM + ROPE_DIM) ** -0.5 * LOG2E)).astype(bf16), kcat[hh])
            s_buf[hh] = s
            m_buf[hh] = jnp.max(s, axis=-1, keepdims=True)

    @pl.when(qi == 0)
    def _():
        kr = _rope(kr_ref[...], cos_ref[...], sup_ref[...], sdn_ref[...]).astype(bf16)
        krc = krc_ref[...].astype(bf16)
        ones = jnp.ones((s_all, V_DIM), bf16)
        for hh in range(2):
            c = slice(hh * hd, (hh + 1) * hd)
            kcat[hh, 0:s_self, 0:hd] = kn_ref[:, c]
            kcat[hh, 0:s_self, hd:2 * hd] = kr
            kcat[hh, s_self:s_all, 0:hd] = knc_ref[:, c]
            kcat[hh, s_self:s_all, hd:2 * hd] = krc
            vaug[hh, 0:s_self, 0:V_DIM] = v_ref[:, c]
            vaug[hh, s_self:s_all, 0:V_DIM] = vc_ref[:, c]
            vaug[hh, :, V_DIM:2 * V_DIM] = ones
        logits(qn_ref, qr_ref, 0, s0, m0)

    for parity in (0, 1):
        (s_cur, m_cur), (s_nxt, m_nxt) = bufs[parity], bufs[1 - parity]

        @pl.when(qi % 2 == parity)
        def _():
            logits(qn_nx_ref, qr_nx_ref, jnp.minimum(qi + 1, nq - 1), s_nxt, m_nxt)
            for hh in range(2):
                e = jnp.exp2(s_cur[hh] - m_cur[hh]).astype(bf16)
                oa = jnp.dot(e, vaug[hh], preferred_element_type=f32)
                o_ref[:, hh * V_DIM:(hh + 1) * V_DIM] = (oa[:, :V_DIM] * (1.0 / oa[:, V_DIM:V_DIM + 1])).astype(bf16)


def _nbr_tables(rows):
    r_tile = min(NBR_ROWS, rows)
    kr = min(WIN_ROWS, rows)
    krw = min(rows, r_tile + kr)
    n_rel = 2 * WIN_ROWS - 1
    ks_list, cfg_list, cfgs = [], [], {}
    for t in range(rows // r_tile):
        qr = t * r_tile + np.arange(r_tile)
        r0 = np.clip(qr - kr // 2, 0, rows - kr)
        ks = int(np.clip(t * r_tile - kr // 2, 0, rows - krw))
        krow = ks + np.arange(krw)
        row_ok = (krow[None, :] >= r0[:, None]) & (krow[None, :] < r0[:, None] + kr)
        rel = np.where(row_ok, krow[None, :] - qr[:, None] + WIN_ROWS - 1, n_rel).astype(np.int32)
        key = rel.tobytes()
        if key not in cfgs:
            cfgs[key] = (len(cfgs), rel)
        ks_list.append(ks)
        cfg_list.append(cfgs[key][0])
    rel_all = np.stack([c[1] for c in sorted(cfgs.values(), key=lambda c: c[0])])
    return r_tile, krw, np.asarray(ks_list, np.int32), np.asarray(cfg_list, np.int32), rel_all


def _nbr_bias(rpb, rel_all):
    w = GRID_W
    kc = min(WIN_COLS, w)
    h, n_rel, _ = rpb.shape
    span = 2 * w - 1
    off = np.clip(np.arange(span) - (w - 1), -(WIN_COLS - 1), WIN_COLS - 1) + WIN_COLS - 1
    v = rpb[:, :, off].astype(f32) * LOG2E
    m = jnp.tile(v, (1, 1, w + 1))[:, :, :w * (span + 1)].reshape(h, n_rel, w, span + 1)[:, :, :, :w]
    toep = m[:, :, ::-1, :]
    cols = np.arange(w)
    col_start = np.clip(cols - kc // 2, 0, w - kc)
    col_ok = (cols[None, :] >= col_start[:, None]) & (cols[None, :] < col_start[:, None] + kc)
    toep = jnp.where(col_ok, toep, NEG_INF)
    toep = jnp.concatenate([toep, jnp.full((h, 1, w, w), NEG_INF, f32)], axis=1)
    n_cfg, r_tile, krw = rel_all.shape
    b = toep[:, rel_all.reshape(-1)].reshape(h, n_cfg, r_tile, krw, w, w)
    return b.transpose(0, 1, 2, 4, 3, 5).reshape(h, n_cfg, r_tile * w, krw * w)


def _attn_even(p, o_width, lam, lam_init, g_sub, rpb, ck_a, cv_a, ck_b, cv_b, dims, tables):
    bc, seq, bs, s_lat, past = dims
    t = p.shape[0]
    tc = bc * seq
    h_a = ck_a.shape[-1] // HEAD_DIM
    h_b = ck_b.shape[-1] // HEAD_DIM
    cos, sup, sdn = tables
    hd = HEAD_DIM
    lam_arr = jnp.reshape(lam, (1,)).astype(f32)
    g2 = g_sub.reshape(1, hd)
    smem = pl.BlockSpec(memory_space=pltpu.SMEM)
    o_shape = jax.ShapeDtypeStruct((t, o_width), bf16)

    nh = _pick(h_a, CTX_HEADS)
    ga = h_a // nh
    wb = nh * hd
    o = pl.pallas_call(
        functools.partial(_diff_ctx_kernel, nh=nh, lam_init=lam_init),
        grid=(bc, ga),
        in_specs=[smem,
                  pl.BlockSpec((seq, wb), lambda b, h: (b, h)),
                  pl.BlockSpec((seq, wb), lambda b, h: (b, ga + h)),
                  pl.BlockSpec((seq, wb), lambda b, h: (b, 2 * ga + h)),
                  pl.BlockSpec((1, hd), lambda b, h: (0, 0))],
        out_specs=pl.BlockSpec((seq, wb), lambda b, h: (b, h)),
        out_shape=o_shape,
        compiler_params=_cp("parallel", "parallel"),
        name="attn_diff_ctx",
    )(lam_arr, p, p, p, g2)

    nhb = _pick(h_b, CTX_HEADS)
    gb = h_b // nhb
    wbb = nhb * hd
    base = 3 * h_a * hd // wbb
    o = pl.pallas_call(
        lambda o_in, q, k, v, o_out: _soft_ctx_kernel(q, k, v, o_out, nh=nhb),
        grid=(bc, gb),
        in_specs=[pl.BlockSpec(memory_space=pl.ANY),
                  pl.BlockSpec((seq, wbb), lambda b, h: (b, base + h)),
                  pl.BlockSpec((seq, wbb), lambda b, h: (b, base + gb + h)),
                  pl.BlockSpec((seq, wbb), lambda b, h: (b, base + 2 * gb + h))],
        out_specs=pl.BlockSpec((seq, wbb), lambda b, h: (b, h_a * hd // wbb + h)),
        out_shape=o_shape,
        input_output_aliases={0: 0},
        compiler_params=_cp("parallel", "parallel"),
        name="attn_soft_ctx",
    )(o, p, p, p)

    tq = _pick(s_lat, Q_TILE)
    nq = s_lat // tq
    full = lambda b, h, i: (0, 0)
    o = pl.pallas_call(
        lambda o_in, *refs: _diff_lat_kernel(*refs, tq=tq, s_self=s_lat, s_cache=past, nq=nq, lam_init=lam_init),
        grid=(bs, h_a, nq),
        in_specs=[pl.BlockSpec(memory_space=pl.ANY), smem,
                  pl.BlockSpec((tq, hd), lambda b, h, i: (tc // tq + b * nq + i, h)),
                  pl.BlockSpec((tq, hd), lambda b, h, i: (tc // tq + b * nq + jnp.minimum(i + 1, nq - 1), h)),
                  pl.BlockSpec((s_lat, hd), lambda b, h, i: (tc // s_lat + b, h_a + h)),
                  pl.BlockSpec((s_lat, hd), lambda b, h, i: (tc // s_lat + b, 2 * h_a + h)),
                  pl.BlockSpec((None, past, hd), lambda b, h, i: (b, 0, h)),
                  pl.BlockSpec((None, past, hd), lambda b, h, i: (b, 0, h)),
                  pl.BlockSpec((s_lat, hd), full), pl.BlockSpec((s_lat, hd), full), pl.BlockSpec((s_lat, hd), full),
                  pl.BlockSpec((1, hd), full)],
        out_specs=pl.BlockSpec((tq, hd), lambda b, h, i: (tc // tq + b * nq + i, h)),
        out_shape=o_shape,
        scratch_shapes=[pltpu.VMEM((s_lat + past, hd), bf16), pltpu.VMEM((s_lat + past, 2 * hd), bf16),
                        pltpu.VMEM((2, tq, s_lat + past), f32), pltpu.VMEM((2, tq, s_lat + past), f32),
                        pltpu.VMEM((2, tq, 1), f32), pltpu.VMEM((2, tq, 1), f32)],
        input_output_aliases={0: 0},
        compiler_params=_cp("parallel", "parallel", "arbitrary"),
        name="attn_diff_lat",
    )(o, lam_arr, p, p, p, p, ck_a, cv_a, cos, sup, sdn, g2)

    rows = s_lat // GRID_W
    r_tile, krw, ks_np, cfg_np, rel_all = _nbr_tables(rows)
    tqn, kw = r_tile * GRID_W, krw * GRID_W
    nqn = rows // r_tile
    bias = _nbr_bias(rpb, rel_all)
    nhn = _pick(math.gcd(h_a, h_b), NBR_HEADS)
    wn = nhn * hd
    gn, ga_n = h_b // nhn, h_a // nhn
    o = pl.pallas_call(
        lambda ks, cfg, o_in, *refs: _nbr_kernel(ks, cfg, *refs, kw=kw, nh=nhn),
        grid_spec=pltpu.PrefetchScalarGridSpec(
            num_scalar_prefetch=2,
            grid=(bs, gn, nqn),
            in_specs=[pl.BlockSpec(memory_space=pl.ANY),
                      pl.BlockSpec((tqn, wn), lambda b, h, i, ks, cfg: (tc // tqn + b * nqn + i, 3 * ga_n + h)),
                      pl.BlockSpec((s_lat, wn), lambda b, h, i, ks, cfg: (tc // s_lat + b, 3 * ga_n + gn + h)),
                      pl.BlockSpec((s_lat, wn), lambda b, h, i, ks, cfg: (tc // s_lat + b, 3 * ga_n + 2 * gn + h)),
                      pl.BlockSpec((None, past, wn), lambda b, h, i, ks, cfg: (b, 0, h)),
                      pl.BlockSpec((None, past, wn), lambda b, h, i, ks, cfg: (b, 0, h)),
                      pl.BlockSpec((nhn, None, tqn, kw), lambda b, h, i, ks, cfg: (h, cfg[i], 0, 0))],
            out_specs=pl.BlockSpec((tqn, wn), lambda b, h, i, ks, cfg: (tc // tqn + b * nqn + i, ga_n + h)),
        ),
        out_shape=o_shape,
        input_output_aliases={2: 0},
        compiler_params=_cp("parallel", "parallel", "arbitrary"),
        name="attn_nbr_lat",
    )(jnp.asarray(ks_np), jnp.asarray(cfg_np), o, p, p, p, ck_b, cv_b, bias)
    return o


def _attn_mla(q, kv, krd, krd_cache, h_c, dims, tables):
    bc, seq, bs, s_lat, past = dims
    t = q.shape[0]
    tc = bc * seq
    cos, sup, sdn = tables
    hp = h_c // 2
    w2 = 2 * NOPE_DIM
    o_shape = jax.ShapeDtypeStruct((t, h_c * V_DIM), bf16)

    npair = _pick(hp, CTX_HEADS // 2)
    gp = hp // npair
    wn, wr = npair * w2, npair * LANES
    o = pl.pallas_call(
        functools.partial(_mla_ctx_kernel, npair=npair),
        grid=(bc, gp),
        in_specs=[pl.BlockSpec((seq, wn), lambda b, h: (b, h)),
                  pl.BlockSpec((seq, wr), lambda b, h: (b, 2 * gp + h)),
                  pl.BlockSpec((seq, wn), lambda b, h: (b, h)),
                  pl.BlockSpec((seq, wn), lambda b, h: (b, gp + h)),
                  pl.BlockSpec((seq, LANES), lambda b, h: (b, 0))],
        out_specs=pl.BlockSpec((seq, wn), lambda b, h: (b, h)),
        out_shape=o_shape,
        compiler_params=_cp("parallel", "parallel"),
        name="attn_mla_ctx",
    )(q, q, kv, kv, krd)

    tq = _pick(s_lat, Q_TILE)
    nq = s_lat // tq
    full = lambda b, h, i: (0, 0)
    o = pl.pallas_call(
        lambda o_in, *refs: _mla_lat_kernel(*refs, tq=tq, s_self=s_lat, s_cache=past, nq=nq),
        grid=(bs, hp, nq),
        in_specs=[pl.BlockSpec(memory_space=pl.ANY),
                  pl.BlockSpec((tq, w2), lambda b, h, i: (tc // tq + b * nq + i, h)),
                  pl.BlockSpec((tq, LANES), lambda b, h, i: (tc // tq + b * nq + i, 2 * hp + h)),
                  pl.BlockSpec((tq, w2), lambda b, h, i: (tc // tq + b * nq + jnp.minimum(i + 1, nq - 1), h)),
                  pl.BlockSpec((tq, LANES),
                               lambda b, h, i: (tc // tq + b * nq + jnp.minimum(i + 1, nq - 1), 2 * hp + h)),
                  pl.BlockSpec((s_lat, w2), lambda b, h, i: (tc // s_lat + b, h)),
                  pl.BlockSpec((s_lat, w2), lambda b, h, i: (tc // s_lat + b, hp + h)),
                  pl.BlockSpec((s_lat, LANES), lambda b, h, i: (tc // s_lat + b, 0)),
                  pl.BlockSpec((past, w2), lambda b, h, i: (t // past + b, h)),
                  pl.BlockSpec((past, w2), lambda b, h, i: (t // past + b, hp + h)),
                  pl.BlockSpec((None, past, LANES), lambda b, h, i: (b, 0, 0)),
                  pl.BlockSpec((s_lat, LANES), full), pl.BlockSpec((s_lat, LANES), full),
                  pl.BlockSpec((s_lat, LANES), full)],
        out_specs=pl.BlockSpec((tq, w2), lambda b, h, i: (tc // tq + b * nq + i, h)),
        out_shape=o_shape,
        scratch_shapes=[pltpu.VMEM((2, s_lat + past, w2), bf16), pltpu.VMEM((2, s_lat + past, 2 * V_DIM), bf16),
                        pltpu.VMEM((2, tq, s_lat + past), f32), pltpu.VMEM((2, tq, s_lat + past), f32),
                        pltpu.VMEM((2, tq, 1), f32), pltpu.VMEM((2, tq, 1), f32)],
        input_output_aliases={0: 0},
        compiler_params=_cp("parallel", "parallel", "arbitrary"),
        name="attn_mla_lat",
    )(o, q, q, q, q, kv, kv, krd, kv, kv, krd_cache, cos, sup, sdn)
    return o


def _router_kernel(h_ref, w_ref, b_ref, eid_ref, wgt_ref, cnt_ref):
    logits = jnp.dot(h_ref[...], w_ref[...], preferred_element_type=f32) + b_ref[...]
    lane = lax.broadcasted_iota(jnp.int32, logits.shape, 1)
    big = jnp.int32(LANES)
    neg = -jnp.inf

    def top(vals):
        v = jnp.max(vals, axis=-1, keepdims=True)
        return v, jnp.min(jnp.where(vals == v, lane, big), axis=-1, keepdims=True)

    gl = jnp.where(lane < N_GROUPS, logits, neg)
    gmax, g_idx = top(gl)
    p_group = 1.0 / jnp.sum(jnp.exp(gl - gmax), axis=-1, keepdims=True)
    lo = N_GROUPS + g_idx * EXPERTS_PER_GROUP
    el = jnp.where((lane >= lo) & (lane < lo + EXPERTS_PER_GROUP), logits, neg)
    v1, i1 = top(el)
    v2, i2 = top(jnp.where(lane == i1, neg, el))
    e = jnp.exp(v2 - v1)
    w1 = p_group / (1.0 + e)
    eid_ref[...] = jnp.where(lane == 0, i1 - N_GROUPS, jnp.where(lane == 1, i2 - N_GROUPS, 0))
    wgt_ref[...] = jnp.where(lane == 0, w1, jnp.where(lane == 1, w1 * e, 0.0))
    cnt_ref[...] = jnp.sum(((lane == i1) | (lane == i2)).astype(jnp.int32), axis=0, keepdims=True)


def _router(h, w_r, b_r):
    t, d = h.shape
    tm = _pick(t, ROW_TILE)
    return pl.pallas_call(
        _router_kernel,
        grid=(t // tm,),
        in_specs=[pl.BlockSpec((tm, d), lambda i: (i, 0)),
                  pl.BlockSpec((d, LANES), lambda i: (0, 0)),
                  pl.BlockSpec((1, LANES), lambda i: (0, 0))],
        out_specs=[pl.BlockSpec((tm, LANES), lambda i: (i, 0)), pl.BlockSpec((tm, LANES), lambda i: (i, 0)),
                   pl.BlockSpec((None, 1, LANES), lambda i: (i, 0, 0))],
        out_shape=[jax.ShapeDtypeStruct((t, LANES), jnp.int32), jax.ShapeDtypeStruct((t, LANES), f32),
                   jax.ShapeDtypeStruct((t // tm, 1, LANES), jnp.int32)],
        compiler_params=_cp("parallel"),
        name="router",
    )(h, w_r, b_r)


def _moe_kernel(te_ref, nu_ref, tok_ref, dst_ref, hp_ref, rw_ref, w1_ref, w3_ref, w2_ref, y_ref,
                xg0, xg1, yb0, yb1, gsem, ssem, *, tm):
    del te_ref
    i = pl.program_id(0)
    n_used = nu_ref[0]
    xg, yb = (xg0, xg1), (yb0, yb1)

    def gather_copy(row, r, slot):
        return pltpu.make_async_copy(hp_ref.at[pl.ds(row, 1)], xg[slot].at[pl.ds(r, 1)], gsem.at[slot])

    def scatter_copy(row, r, slot):
        return pltpu.make_async_copy(yb[slot].at[pl.ds(r, 1)], y_ref.at[pl.ds(row, 1)], ssem.at[slot])

    def gather_wait(slot):
        pltpu.make_async_copy(hp_ref.at[pl.ds(0, tm)], xg[slot], gsem.at[slot]).wait()

    def scatter_wait(slot):
        pltpu.make_async_copy(yb[slot], y_ref.at[pl.ds(0, tm)], ssem.at[slot]).wait()

    def compute(slot):
        u = xg[slot][...]
        half = u.shape[1]
        x_lo = lax.bitcast_convert_type(u << 16, f32).astype(bf16)
        x_hi = lax.bitcast_convert_type(u & jnp.uint32(0xFFFF0000), f32).astype(bf16)
        a = (jnp.dot(x_lo, w1_ref[0:half, :], preferred_element_type=f32)
             + jnp.dot(x_hi, w1_ref[half:2 * half, :], preferred_element_type=f32))
        b = (jnp.dot(x_lo, w3_ref[0:half, :], preferred_element_type=f32)
             + jnp.dot(x_hi, w3_ref[half:2 * half, :], preferred_element_type=f32))
        hid = (jax.nn.silu(a) * b * rw_ref[...]).astype(bf16)
        yb[slot][...] = jnp.dot(hid, w2_ref[...], preferred_element_type=f32)

    @pl.when(i == 0)
    def _():
        yb1[...] = jnp.zeros_like(yb1)

        def prime(r, carry):
            gather_copy(tok_ref[r], r, 0).start()
            return carry

        lax.fori_loop(0, tm, prime, 0)

    for parity in (0, 1):
        cur, nxt = parity, 1 - parity

        @pl.when((i % 2 == parity) & (i >= 1) & (i <= n_used))
        def _():
            scatter_wait(cur)

        @pl.when((i % 2 == parity) & (i < n_used))
        def _():
            gather_wait(cur)
            for r in range(tm):
                gather_copy(tok_ref[(i + 1) * tm + r], r, nxt).start()
            for r in range(tm):
                scatter_copy(dst_ref[i * tm + r], r, nxt).start()

        @pl.when((i % 2 == parity) & (i + 1 <= n_used))
        def _():
            compute(cur)

        @pl.when((i % 2 == parity) & (i == n_used))
        def _():
            gather_wait(cur)

            def flush(r, carry):
                scatter_copy(dst_ref[i * tm + r], r, nxt).start()
                return carry

            lax.fori_loop(0, tm, flush, 0)
            scatter_wait(nxt)


def _moe(hp, eid, wgt, counts, w1, w3, w2):
    t, half = hp.shape
    d = 2 * half
    n_exp, _, f = w1.shape
    tm = MOE_TILE
    n_assign = 2 * t
    n_tiles = n_assign // tm + n_exp + 1
    n_rows = n_tiles * tm

    e_flat = eid.T.reshape(-1)
    w_flat = wgt.T.reshape(-1)
    order = jnp.argsort(e_flat, stable=True).astype(jnp.int32)
    padded = ((counts + tm - 1) // tm) * tm
    pad_end = jnp.cumsum(padded)
    pad_start = pad_end - padded
    src_start = jnp.cumsum(counts) - counts
    n_used = (pad_end[-1] // tm).astype(jnp.int32).reshape(1)
    tile_start = jnp.arange(n_tiles, dtype=jnp.int32) * tm
    tile_e = jnp.minimum(jnp.sum((pad_end[None, :] <= tile_start[:, None]).astype(jnp.int32), axis=1), n_exp - 1)
    rows = jnp.arange(n_rows, dtype=jnp.int32)
    row_e = jnp.broadcast_to(tile_e[:, None], (n_tiles, tm)).reshape(-1)
    j = rows - pad_start[row_e]
    valid = (j < counts[row_e]) & (rows < pad_end[-1])
    a = order[jnp.clip(src_start[row_e] + j, 0, n_assign - 1)]
    spare = n_assign + rows % tm
    row_tok = jnp.where(valid, a % t, 0).astype(jnp.int32)
    row_dst = jnp.concatenate([spare[:tm], jnp.where(valid, a, spare)]).astype(jnp.int32)
    row_w = jnp.where(valid, w_flat[a], 0.0).astype(f32).reshape(n_rows, 1)

    return pl.pallas_call(
        functools.partial(_moe_kernel, tm=tm),
        grid_spec=pltpu.PrefetchScalarGridSpec(
            num_scalar_prefetch=4,
            grid=(n_tiles,),
            in_specs=[pl.BlockSpec(memory_space=pl.ANY),
                      pl.BlockSpec((tm, 1), lambda i, te, nu, tok, dst: (i, 0)),
                      pl.BlockSpec((None, d, f), lambda i, te, nu, tok, dst: (te[i], 0, 0)),
                      pl.BlockSpec((None, d, f), lambda i, te, nu, tok, dst: (te[i], 0, 0)),
                      pl.BlockSpec((None, f, d), lambda i, te, nu, tok, dst: (te[i], 0, 0))],
            out_specs=pl.BlockSpec(memory_space=pl.ANY),
            scratch_shapes=[pltpu.VMEM((tm, half), jnp.uint32), pltpu.VMEM((tm, half), jnp.uint32),
                            pltpu.VMEM((tm, d), f32), pltpu.VMEM((tm, d), f32),
                            pltpu.SemaphoreType.DMA((2,)), pltpu.SemaphoreType.DMA((2,))],
        ),
        out_shape=jax.ShapeDtypeStruct((n_assign + tm, d), f32),
        compiler_params=_cp("arbitrary"),
        name="moe",
    )(tile_e, n_used, row_tok, row_dst, hp, row_w, w1, w3, w2)


def kernel(x_prompt, x_sample, c, c_ctx, cache_a_k, cache_a_v, cache_b_k, cache_b_v, cache_c_kv, cache_c_kr,
           g_norm1, g_norm2, g_final, w_ada, b_ada, w_in_ab, w_out_ab, lam_q1, lam_k1, lam_q2, lam_k2,
           g_sub_a, rpb_b, w_down_c, g_q_c, g_kv_c, w_uq_c, w_uk_c, w_uv_c, w_out_c,
           w_group_router, b_group_router, w_expert_router, b_expert_router, w1_moe, w3_moe, w2_moe):
    bc, seq, d = x_prompt.shape
    bs, s_lat, _ = x_sample.shape
    past = cache_a_k.shape[2]
    depth = w_ada.shape[0]
    tc, ts = bc * seq, bs * s_lat
    t = tc + ts
    h_a, h_b = cache_a_k.shape[3], cache_b_k.shape[3]
    ab = (h_a + h_b) * HEAD_DIM
    dims = (bc, seq, bs, s_lat, past)
    assert tc % s_lat == 0 and t % past == 0 and s_lat % (2 * ROW_TILE) == 0

    x = jnp.concatenate([x_prompt.reshape(tc, d), x_sample.reshape(ts, d)], axis=0)
    cond = jnp.concatenate([c_ctx[None], c, jnp.zeros((COND_ROWS_PAD - 1 - bs, d), f32)], axis=0)
    mod = _ada_mod(cond, w_ada, b_ada)
    tables = _rope_tables(s_lat)

    n_exp = N_GROUPS * EXPERTS_PER_GROUP
    sak, sav, sbk, sbv, sckv, sckr = [], [], [], [], [], []
    moe_pending = None
    for l in range(depth):
        mod4 = mod[l].reshape(COND_ROWS_PAD, 6, 1, d)
        if moe_pending is None:
            h = _norm(x, g_norm1[l], mod4, tc, s_lat, sel=0)[0]
        else:
            x, h = _norm(x, g_norm1[l], mod4, tc, s_lat, sel=0, moe=moe_pending)
        if l % 2 == 0:
            e = l // 2
            lam_init = 0.8 - 0.6 * math.exp(-0.3 * l)
            lam = (jnp.exp(jnp.sum((lam_q1[e] * lam_k1[e]).astype(f32)))
                   - jnp.exp(jnp.sum((lam_q2[e] * lam_k2[e]).astype(f32))) + lam_init)
            p = _matmul(h, w_in_ab[e].astype(bf16), f32)
            wa = h_a * HEAD_DIM
            sak.append(p[:tc, wa:2 * wa].reshape(bc, seq, h_a, HEAD_DIM))
            sav.append(p[:tc, 2 * wa:3 * wa].reshape(bc, seq, h_a, HEAD_DIM))
            wb = h_b * HEAD_DIM
            sbk.append(p[:tc, 3 * wa + wb:3 * wa + 2 * wb].reshape(bc, seq, h_b, HEAD_DIM))
            sbv.append(p[:tc, 3 * wa + 2 * wb:].reshape(bc, seq, h_b, HEAD_DIM))
            o = _attn_even(p, ab, lam, lam_init, g_sub_a[e], rpb_b[e],
                           cache_a_k[:, e].reshape(bs, past, wa), cache_a_v[:, e].reshape(bs, past, wa),
                           cache_b_k[:, e].reshape(bs, past, wb), cache_b_v[:, e].reshape(bs, past, wb),
                           dims, tables)
            w_out = w_out_ab[e].astype(bf16)
        else:
            oi = l // 2
            q_lora, kv_lora = g_q_c.shape[1], g_kv_c.shape[1]
            h_c = w_uk_c.shape[2] // NOPE_DIM
            wd = w_down_c[oi]
            wd = jnp.concatenate([wd, wd[:, q_lora + kv_lora:]], axis=1).astype(bf16)
            cq, ckv, krd = _mla_down(h, wd, g_q_c[oi], g_kv_c[oi])
            sckv.append(ckv[:tc].reshape(bc, seq, kv_lora))
            sckr.append(krd[:tc, :ROPE_DIM].reshape(bc, seq, ROPE_DIM))
            wq = w_uq_c[oi].reshape(q_lora, h_c, NOPE_DIM + ROPE_DIM)
            wq = jnp.concatenate([wq[:, :, :NOPE_DIM].reshape(q_lora, -1), wq[:, :, NOPE_DIM:].reshape(q_lora, -1)],
                                 axis=1).astype(bf16)
            q = _matmul(cq, wq, f32)
            ckv_all = jnp.concatenate([ckv, cache_c_kv[:, oi].reshape(bs * past, kv_lora)], axis=0).astype(bf16)
            wkv = jnp.concatenate([w_uk_c[oi], w_uv_c[oi]], axis=1).astype(bf16)
            kv = _matmul(ckv_all, wkv, bf16)
            krc = cache_c_kr[:, oi]
            o = _attn_mla(q, kv, krd, jnp.concatenate([krc, krc], axis=-1), h_c, dims, tables)
            w_out = w_out_c[oi].astype(bf16)
        x = _matmul(o, w_out, f32, residual=(x, mod4, 2, tc, s_lat))
        h2, hp = _norm(x, g_norm2[l], mod4, tc, s_lat, sel=1, packed=True)
        w_r = jnp.concatenate([w_group_router[l], w_expert_router[l]], axis=1)
        w_r = jnp.pad(w_r, ((0, 0), (0, LANES - w_r.shape[1]))).astype(bf16)
        b_r = jnp.pad(jnp.concatenate([b_group_router[l], b_expert_router[l]]), (0, LANES - N_GROUPS - n_exp))
        eid, wgt, cnt = _router(h2, w_r, b_r.reshape(1, LANES).astype(f32))
        counts = jnp.sum(cnt, axis=(0, 1))[N_GROUPS:N_GROUPS + n_exp]
        f = w1_moe.shape[-1]
        y = _moe(hp, eid[:, :2], wgt[:, :2], counts,
                 w1_moe[l].reshape(n_exp, d, f).astype(bf16), w3_moe[l].reshape(n_exp, d, f).astype(bf16),
                 w2_moe[l].reshape(n_exp, f, d).astype(bf16))
        moe_pending = (y, mod4, 5)
    out = _norm(x, g_final, None, tc, s_lat, moe=moe_pending, final=True)[0]
    y_prompt = out[:tc].reshape(bc, seq, d)
    y_sample = out[tc:].reshape(bs, s_lat, d)
    return (y_prompt, y_sample, jnp.stack(sak, axis=1), jnp.stack(sav, axis=1), jnp.stack(sbk, axis=1),
            jnp.stack(sbv, axis=1), jnp.stack(sckv, axis=1), jnp.stack(sckr, axis=1))
```

```python
import functools
import math

import numpy as np
import jax
import jax.numpy as jnp
from jax import lax
from jax.experimental import pallas as pl
from jax.experimental.pallas import tpu as pltpu

GRID_W = 64
HEAD_DIM = 128
WIN_ROWS = 8
WIN_COLS = 16
NOPE_DIM = 128
ROPE_DIM = 64
V_DIM = 128
N_GROUPS = 4
EXPERTS_PER_GROUP = 8
ROPE_THETA = 10000.0
NORM_EPS = 1e-6
NEG_INF = -1e30
LOG2E = math.log2(math.e)

LANES = 128
COND_ROWS_PAD = 16
VMEM_LIMIT = 56 * 1024 * 1024
ROW_TILE = 512
COL_TILE = 512
NORM_TILE = 256
MOE_TILE = 256
Q_TILE = 512
NBR_ROWS = 4
CTX_HEADS = 8
NBR_HEADS = 2

f32 = jnp.float32
bf16 = jnp.bfloat16


def _cp(*sem):
    return pltpu.CompilerParams(dimension_semantics=sem, vmem_limit_bytes=VMEM_LIMIT)


def _pick(n, pref):
    if n <= pref:
        return n
    t = pref
    while n % t:
        t //= 2
    return t


def _cond_row(r0, tc, s_lat):
    return jnp.where(r0 < tc, 0, 1 + (r0 - tc) // s_lat)


def _dot_nt(a, b):
    return lax.dot_general(a, b, (((1,), (1,)), ((), ())), preferred_element_type=f32)


def _ada_kernel(c_ref, w_ref, b_ref, o_ref):
    s = jax.nn.silu(c_ref[...]).astype(bf16)
    o_ref[...] = jnp.dot(s, w_ref[...].astype(bf16), preferred_element_type=f32) + b_ref[...]


def _ada_mod(cond, w_ada, b_ada):
    depth, d, n = w_ada.shape
    r = cond.shape[0]
    tn = _pick(n, COL_TILE)
    return pl.pallas_call(
        _ada_kernel,
        grid=(depth, n // tn),
        in_specs=[pl.BlockSpec((r, d), lambda l, j: (0, 0)),
                  pl.BlockSpec((None, d, tn), lambda l, j: (l, 0, j)),
                  pl.BlockSpec((None, 1, tn), lambda l, j: (l, 0, j))],
        out_specs=pl.BlockSpec((None, r, tn), lambda l, j: (l, 0, j)),
        out_shape=jax.ShapeDtypeStruct((depth, r, n), f32),
        compiler_params=_cp("parallel", "parallel"),
        name="ada_mod",
    )(cond, w_ada, b_ada.reshape(depth, 1, n))


def _norm_kernel(*refs, combine, final, route):
    it = iter(refs)
    x_ref = next(it)
    if combine:
        y0_ref, y1_ref, gate_ref = next(it), next(it), next(it)
    g_ref = next(it)
    if not final:
        shift_ref, scale_ref = next(it), next(it)
    if route:
        wr_ref, br_ref = next(it), next(it)
    x = x_ref[...]
    if combine:
        x = x + gate_ref[...] * (y0_ref[...] + y1_ref[...])
        if not final:
            next(it)[...] = x
    y = x * lax.rsqrt(jnp.mean(x * x, axis=-1, keepdims=True) + NORM_EPS) * g_ref[...]
    if final:
        next(it)[...] = y
        return
    h = (y * (1.0 + scale_ref[...]) + shift_ref[...]).astype(bf16)
    if not route:
        next(it)[...] = h
        return
    bits = lax.bitcast_convert_type(h.astype(f32), jnp.uint32)
    half = bits.shape[1] // 2
    next(it)[...] = (bits[:, half:] & jnp.uint32(0xFFFF0000)) | (bits[:, :half] >> 16)
    eid, wgt, cnt = _route(jnp.dot(h, wr_ref[...], preferred_element_type=f32) + br_ref[...])
    next(it)[...] = eid
    next(it)[...] = wgt
    next(it)[...] = cnt


def _norm(x, g, mod4, tc, s_lat, *, sel=None, moe=None, final=False, route=None, rows=None):
    t, d = x.shape
    tm = NORM_TILE
    nt = t // tm
    combine = moe is not None
    i0, n_steps = (0, nt) if rows is None else (rows[0] // tm, rows[1] // tm)
    assert rows is None or final
    row = lambda i: (i + i0, 0)
    modspec = lambda j: pl.BlockSpec((None, None, 1, d),
                                     lambda i: (_cond_row((i + i0) * tm, tc, s_lat), j, 0, 0))
    args, specs = [x], [pl.BlockSpec((tm, d), row)]
    if combine:
        y, mod4_moe, gate_j = moe
        args += [y, y, mod4_moe]
        specs += [pl.BlockSpec((tm, d), row), pl.BlockSpec((tm, d), lambda i: (i + i0 + nt, 0)), modspec(gate_j)]
    args.append(g.reshape(1, d))
    specs.append(pl.BlockSpec((1, d), lambda i: (0, 0)))
    if not final:
        args += [mod4, mod4]
        specs += [modspec(3 * sel), modspec(3 * sel + 1)]
    if route is not None:
        args += list(route)
        specs += [pl.BlockSpec((d, LANES), lambda i: (0, 0)), pl.BlockSpec((1, LANES), lambda i: (0, 0))]
    shapes, ospecs = [], []
    if combine and not final:
        shapes.append(jax.ShapeDtypeStruct((t, d), f32))
        ospecs.append(pl.BlockSpec((tm, d), row))
    if final:
        shapes.append(jax.ShapeDtypeStruct((n_steps * tm, d), f32))
        ospecs.append(pl.BlockSpec((tm, d), lambda i: (i, 0)))
    elif route is None:
        shapes.append(jax.ShapeDtypeStruct((t, d), bf16))
        ospecs.append(pl.BlockSpec((tm, d), row))
    else:
        shapes += [jax.ShapeDtypeStruct((t, d // 2), jnp.uint32), jax.ShapeDtypeStruct((t, LANES), jnp.int32),
                   jax.ShapeDtypeStruct((t, LANES), f32), jax.ShapeDtypeStruct((nt, 1, LANES), jnp.int32)]
        ospecs += [pl.BlockSpec((tm, d // 2), row), pl.BlockSpec((tm, LANES), row), pl.BlockSpec((tm, LANES), row),
                   pl.BlockSpec((None, 1, LANES), lambda i: (i, 0, 0))]
    return pl.pallas_call(
        functools.partial(_norm_kernel, combine=combine, final=final, route=route is not None),
        grid=(n_steps,),
        in_specs=specs,
        out_specs=ospecs,
        out_shape=shapes,
        input_output_aliases={0: 0} if (combine and not final) else {},
        compiler_params=_cp("parallel"),
        name="norm",
    )(*args)


def _mm_kernel(*refs, residual):
    if residual:
        a_ref, b_ref, x_ref, gate_ref, o_ref = refs
    else:
        a_ref, b_ref, o_ref = refs
    acc = jnp.dot(a_ref[...], b_ref[...], preferred_element_type=f32)
    if residual:
        acc = x_ref[...] + gate_ref[...] * acc
    o_ref[...] = acc.astype(o_ref.dtype)


def _matmul(a, b, out_dtype, *, residual=None):
    m, k = a.shape
    n = b.shape[1]
    tm = _pick(m, 2 * ROW_TILE if (k >= 4 * ROW_TILE or k <= ROW_TILE) else ROW_TILE)
    tn = _pick(n, COL_TILE * max(1, min(4, 4 * ROW_TILE // k)))
    args = [a, b]
    specs = [pl.BlockSpec((tm, k), lambda i, j: (i, 0)), pl.BlockSpec((k, tn), lambda i, j: (0, j))]
    aliases = {}
    if residual is not None:
        x, mod4, gate_j, tc, s_lat = residual
        args += [x, mod4]
        specs += [pl.BlockSpec((tm, tn), lambda i, j: (i, j)),
                  pl.BlockSpec((None, None, 1, tn), lambda i, j: (_cond_row(i * tm, tc, s_lat), gate_j, 0, j))]
        aliases = {2: 0}
    return pl.pallas_call(
        functools.partial(_mm_kernel, residual=residual is not None),
        grid=(m // tm, n // tn),
        in_specs=specs,
        out_specs=pl.BlockSpec((tm, tn), lambda i, j: (i, j)),
        out_shape=jax.ShapeDtypeStruct((m, n), out_dtype),
        input_output_aliases=aliases,
        compiler_params=_cp("parallel", "parallel"),
        name="matmul",
    )(*args)


def _down_kernel(a_ref, w_ref, gq_ref, gkv_ref, cq_ref, ckv_ref, kr_ref, *, q_lora, kv_lora):
    acc = jnp.dot(a_ref[...], w_ref[...], preferred_element_type=f32)
    cq = acc[:, :q_lora]
    cq = cq * lax.rsqrt(jnp.mean(cq * cq, axis=-1, keepdims=True) + NORM_EPS) * gq_ref[...]
    cq_ref[...] = cq.astype(bf16)
    ckv = acc[:, q_lora:q_lora + kv_lora]
    ckv_ref[...] = ckv * lax.rsqrt(jnp.mean(ckv * ckv, axis=-1, keepdims=True) + NORM_EPS) * gkv_ref[...]
    kr_ref[...] = acc[:, q_lora + kv_lora:]


def _mla_down(h, w_down_dup, g_q, g_kv):
    t, d = h.shape
    q_lora, kv_lora = g_q.shape[0], g_kv.shape[0]
    n = w_down_dup.shape[1]
    tm = _pick(t, ROW_TILE)
    return pl.pallas_call(
        functools.partial(_down_kernel, q_lora=q_lora, kv_lora=kv_lora),
        grid=(t // tm,),
        in_specs=[pl.BlockSpec((tm, d), lambda i: (i, 0)),
                  pl.BlockSpec((d, n), lambda i: (0, 0)),
                  pl.BlockSpec((1, q_lora), lambda i: (0, 0)),
                  pl.BlockSpec((1, kv_lora), lambda i: (0, 0))],
        out_specs=[pl.BlockSpec((tm, q_lora), lambda i: (i, 0)),
                   pl.BlockSpec((tm, kv_lora), lambda i: (i, 0)),
                   pl.BlockSpec((tm, LANES), lambda i: (i, 0))],
        out_shape=[jax.ShapeDtypeStruct((t, q_lora), bf16),
                   jax.ShapeDtypeStruct((t, kv_lora), f32),
                   jax.ShapeDtypeStruct((t, LANES), f32)],
        compiler_params=_cp("parallel"),
        name="mla_down",
    )(h, w_down_dup, g_q.reshape(1, -1), g_kv.reshape(1, -1))


def _rope_tables(s_lat):
    half = ROPE_DIM // 2
    quarter = half // 2
    tpos = jnp.arange(s_lat)
    inv = ROPE_THETA ** (-jnp.arange(0, half, 2, dtype=f32) / half)
    ang_r = (tpos // GRID_W).astype(f32)[:, None] * inv
    ang_c = (tpos % GRID_W).astype(f32)[:, None] * inv
    ang = jnp.concatenate([ang_r, ang_r, ang_c, ang_c], axis=-1)
    ang = jnp.tile(ang, (1, LANES // ROPE_DIM))
    first = (jnp.arange(LANES) % half) < quarter
    sin = jnp.sin(ang)
    return jnp.cos(ang), jnp.where(first, -sin, 0.0), jnp.where(first, 0.0, sin)


def _rope(x, cos, sin_up, sin_dn):
    quarter = ROPE_DIM // 4
    return x * cos + pltpu.roll(x, LANES - quarter, 1) * sin_up + pltpu.roll(x, quarter, 1) * sin_dn


def _softmax_parts(s):
    m = jnp.max(s, axis=-1, keepdims=True)
    e = jnp.exp2(s - m)
    return e, jnp.sum(e, axis=-1, keepdims=True)


def _diff_logits(q, kk):
    dqk = HEAD_DIM // 2
    q = q * (dqk ** -0.5 * LOG2E)
    lane = lax.broadcasted_iota(jnp.int32, q.shape, 1)
    return (_dot_nt(jnp.where(lane < dqk, q, 0.0).astype(bf16), kk),
            _dot_nt(jnp.where(lane >= dqk, q, 0.0).astype(bf16), kk))


def _diff_finish(s1, m1, s2, m2, vv, lam, g, lam_init):
    e1, e2 = jnp.exp2(s1 - m1), jnp.exp2(s2 - m2)
    l1, l2 = jnp.sum(e1, axis=-1, keepdims=True), jnp.sum(e2, axis=-1, keepdims=True)
    a = e1 * (1.0 / l1) - e2 * (lam / l2)
    o = jnp.dot(a.astype(bf16), vv, preferred_element_type=f32)
    o = o * lax.rsqrt(jnp.mean(o * o, axis=-1, keepdims=True) + NORM_EPS) * g
    return (o * (1.0 - lam_init)).astype(bf16)


def _diff_finish_aug(s1, m1, s2, m2, vaug, lam, g, lam_init):
    o1 = jnp.dot(jnp.exp2(s1 - m1).astype(bf16), vaug, preferred_element_type=f32)
    o2 = jnp.dot(jnp.exp2(s2 - m2).astype(bf16), vaug, preferred_element_type=f32)
    hd = HEAD_DIM
    o = o1[:, :hd] * (1.0 / o1[:, hd:hd + 1]) - o2[:, :hd] * (lam / o2[:, hd:hd + 1])
    o = o * lax.rsqrt(jnp.mean(o * o, axis=-1, keepdims=True) + NORM_EPS) * g
    return (o * (1.0 - lam_init)).astype(bf16)


def _diff_ctx_kernel(lam_ref, q_ref, k_ref, v_ref, g_ref, o_ref, *, nh, lam_init):
    for hh in range(nh):
        cols = slice(hh * HEAD_DIM, (hh + 1) * HEAD_DIM)
        s1, s2 = _diff_logits(q_ref[:, cols], k_ref[:, cols].astype(bf16))
        o_ref[:, cols] = _diff_finish(s1, jnp.max(s1, axis=-1, keepdims=True), s2, jnp.max(s2, axis=-1, keepdims=True),
                                      v_ref[:, cols].astype(bf16), lam_ref[0], g_ref[...], lam_init)


def _diff_lat_kernel(lam_ref, q_ref, q_nx_ref, k_ref, v_ref, kc_ref, vc_ref, cos_ref, sup_ref, sdn_ref, g_ref, o_ref,
                     kall, vall, s0, s1, m0, m1, *, tq, s_self, s_cache, nq, lam_init):
    qi = pl.program_id(2)
    bufs = ((s0, m0), (s1, m1))

    def logits(q_r, tile, s_buf, m_buf):
        rows = pl.ds(pl.multiple_of(tile * tq, tq), tq)
        q = _rope(q_r[...], cos_ref[rows, :], sup_ref[rows, :], sdn_ref[rows, :])
        for c, s in enumerate(_diff_logits(q, kall[...])):
            s_buf[c] = s
            m_buf[c] = jnp.max(s, axis=-1, keepdims=True)

    @pl.when(qi == 0)
    def _():
        kall[0:s_self, :] = _rope(k_ref[...], cos_ref[...], sup_ref[...], sdn_ref[...]).astype(bf16)
        kall[s_self:s_self + s_cache, :] = kc_ref[...].astype(bf16)
        vall[0:s_self, 0:HEAD_DIM] = v_ref[...].astype(bf16)
        vall[s_self:s_self + s_cache, 0:HEAD_DIM] = vc_ref[...].astype(bf16)
        vall[:, HEAD_DIM:2 * HEAD_DIM] = jnp.ones((s_self + s_cache, HEAD_DIM), bf16)
        logits(q_ref, 0, s0, m0)

    for parity in (0, 1):
        (s_cur, m_cur), (s_nxt, m_nxt) = bufs[parity], bufs[1 - parity]

        @pl.when(qi % 2 == parity)
        def _():
            logits(q_nx_ref, jnp.minimum(qi + 1, nq - 1), s_nxt, m_nxt)
            o_ref[...] = _diff_finish_aug(s_cur[0], m_cur[0], s_cur[1], m_cur[1], vall[...], lam_ref[0],
                                          g_ref[...], lam_init)


def _soft_ctx_kernel(q_ref, k_ref, v_ref, o_ref, *, nh):
    for hh in range(nh):
        cols = slice(hh * HEAD_DIM, (hh + 1) * HEAD_DIM)
        q = (q_ref[:, cols] * (HEAD_DIM ** -0.5 * LOG2E)).astype(bf16)
        e, l = _softmax_parts(_dot_nt(q, k_ref[:, cols].astype(bf16)))
        o = jnp.dot(e.astype(bf16), v_ref[:, cols].astype(bf16), preferred_element_type=f32)
        o_ref[:, cols] = (o * (1.0 / l)).astype(bf16)


def _nbr_kernel(ks_ref, cfg_ref, q_ref, k_ref, v_ref, kc_ref, vc_ref, bias_ref, o_ref, *, kw, nh):
    del cfg_ref
    t = pl.program_id(2)
    rows = pl.ds(pl.multiple_of(ks_ref[t] * GRID_W, GRID_W), kw)
    for hh in range(nh):
        c = slice(hh * HEAD_DIM, (hh + 1) * HEAD_DIM)
        q = (q_ref[:, c] * (HEAD_DIM ** -0.5 * LOG2E)).astype(bf16)
        s_loc = _dot_nt(q, k_ref[rows, c].astype(bf16)) + bias_ref[hh]
        s_ctx = _dot_nt(q, kc_ref[:, c].astype(bf16))
        m = jnp.maximum(jnp.max(s_loc, axis=-1, keepdims=True), jnp.max(s_ctx, axis=-1, keepdims=True))
        e_loc, e_ctx = jnp.exp2(s_loc - m), jnp.exp2(s_ctx - m)
        l = jnp.sum(e_loc, axis=-1, keepdims=True) + jnp.sum(e_ctx, axis=-1, keepdims=True)
        o = (jnp.dot(e_loc.astype(bf16), v_ref[rows, c].astype(bf16), preferred_element_type=f32)
             + jnp.dot(e_ctx.astype(bf16), vc_ref[:, c].astype(bf16), preferred_element_type=f32))
        o_ref[:, c] = (o * (1.0 / l)).astype(bf16)


def _mla_head(qn, qr_masked, kcat, vv):
    qc = (jnp.concatenate([qn, qr_masked], axis=-1) * ((NOPE_DIM + ROPE_DIM) ** -0.5 * LOG2E)).astype(bf16)
    e, l = _softmax_parts(_dot_nt(qc, kcat))
    o = jnp.dot(e.astype(bf16), vv, preferred_element_type=f32)
    return (o * (1.0 / l)).astype(bf16)


def _mla_ctx_kernel(qn_ref, qr_ref, kn_ref, v_ref, kr_ref, o_ref, *, npair):
    hd = NOPE_DIM
    kr = kr_ref[...].astype(bf16)
    lane = lax.broadcasted_iota(jnp.int32, (qr_ref.shape[0], LANES), 1)
    for pp in range(npair):
        qr = qr_ref[:, pp * LANES:(pp + 1) * LANES]
        for hh in range(2):
            c = slice((2 * pp + hh) * hd, (2 * pp + hh + 1) * hd)
            mask = (lane < ROPE_DIM) if hh == 0 else (lane >= ROPE_DIM)
            kcat = jnp.concatenate([kn_ref[:, c], kr], axis=-1)
            o_ref[:, c] = _mla_head(qn_ref[:, c], jnp.where(mask, qr, 0.0), kcat, v_ref[:, c])


def _mla_lat_kernel(qn_ref, qr_ref, qn_nx_ref, qr_nx_ref, kn_ref, v_ref, kr_ref, knc_ref, vc_ref, krc_ref,
                    cos_ref, sup_ref, sdn_ref, o_ref, kcat, vaug, s0, s1, m0, m1, *, tq, s_self, s_cache, nq):
    qi = pl.program_id(2)
    hd = NOPE_DIM
    s_all = s_self + s_cache
    bufs = ((s0, m0), (s1, m1))

    def logits(qn_r, qr_r, tile, s_buf, m_buf):
        rows = pl.ds(pl.multiple_of(tile * tq, tq), tq)
        qr = _rope(qr_r[...], cos_ref[rows, :], sup_ref[rows, :], sdn_ref[rows, :])
        lane = lax.broadcasted_iota(jnp.int32, qr.shape, 1)
        for hh in range(2):
            mask = (lane < ROPE_DIM) if hh == 0 else (lane >= ROPE_DIM)
            qc = jnp.concatenate([qn_r[:, hh * hd:(hh + 1) * hd], jnp.where(mask, qr, 0.0)], axis=-1)
            s = _dot_nt((qc * ((NOPE_DIM + ROPE_DIM) ** -0.5 * LOG2E)).astype(bf16), kcat[hh])
            s_buf[hh] = s
            m_buf[hh] = jnp.max(s, axis=-1, keepdims=True)

    @pl.when(qi == 0)
    def _():
        kr = _rope(kr_ref[...], cos_ref[...], sup_ref[...], sdn_ref[...]).astype(bf16)
        krc = krc_ref[...].astype(bf16)
        ones = jnp.ones((s_all, V_DIM), bf16)
        for hh in range(2):
            c = slice(hh * hd, (hh + 1) * hd)
            kcat[hh, 0:s_self, 0:hd] = kn_ref[:, c]
            kcat[hh, 0:s_self, hd:2 * hd] = kr
            kcat[hh, s_self:s_all, 0:hd] = knc_ref[:, c]
            kcat[hh, s_self:s_all, hd:2 * hd] = krc
            vaug[hh, 0:s_self, 0:V_DIM] = v_ref[:, c]
            vaug[hh, s_self:s_all, 0:V_DIM] = vc_ref[:, c]
            vaug[hh, :, V_DIM:2 * V_DIM] = ones
        logits(qn_ref, qr_ref, 0, s0, m0)

    for parity in (0, 1):
        (s_cur, m_cur), (s_nxt, m_nxt) = bufs[parity], bufs[1 - parity]

        @pl.when(qi % 2 == parity)
        def _():
            logits(qn_nx_ref, qr_nx_ref, jnp.minimum(qi + 1, nq - 1), s_nxt, m_nxt)
            for hh in range(2):
                e = jnp.exp2(s_cur[hh] - m_cur[hh]).astype(bf16)
                oa = jnp.dot(e, vaug[hh], preferred_element_type=f32)
                o_ref[:, hh * V_DIM:(hh + 1) * V_DIM] = (oa[:, :V_DIM] * (1.0 / oa[:, V_DIM:V_DIM + 1])).astype(bf16)


def _nbr_tables(rows):
    r_tile = min(NBR_ROWS, rows)
    kr = min(WIN_ROWS, rows)
    krw = min(rows, r_tile + kr)
    n_rel = 2 * WIN_ROWS - 1
    ks_list, cfg_list, cfgs = [], [], {}
    for t in range(rows // r_tile):
        qr = t * r_tile + np.arange(r_tile)
        r0 = np.clip(qr - kr // 2, 0, rows - kr)
        ks = int(np.clip(t * r_tile - kr // 2, 0, rows - krw))
        krow = ks + np.arange(krw)
        row_ok = (krow[None, :] >= r0[:, None]) & (krow[None, :] < r0[:, None] + kr)
        rel = np.where(row_ok, krow[None, :] - qr[:, None] + WIN_ROWS - 1, n_rel).astype(np.int32)
        key = rel.tobytes()
        if key not in cfgs:
            cfgs[key] = (len(cfgs), rel)
        ks_list.append(ks)
        cfg_list.append(cfgs[key][0])
    rel_all = np.stack([c[1] for c in sorted(cfgs.values(), key=lambda c: c[0])])
    return r_tile, krw, np.asarray(ks_list, np.int32), np.asarray(cfg_list, np.int32), rel_all


def _nbr_bias(rpb, rel_all):
    w = GRID_W
    kc = min(WIN_COLS, w)
    h, n_rel, _ = rpb.shape
    span = 2 * w - 1
    off = np.clip(np.arange(span) - (w - 1), -(WIN_COLS - 1), WIN_COLS - 1) + WIN_COLS - 1
    v = rpb[:, :, off].astype(f32) * LOG2E
    m = jnp.tile(v, (1, 1, w + 1))[:, :, :w * (span + 1)].reshape(h, n_rel, w, span + 1)[:, :, :, :w]
    toep = m[:, :, ::-1, :]
    cols = np.arange(w)
    col_start = np.clip(cols - kc // 2, 0, w - kc)
    col_ok = (cols[None, :] >= col_start[:, None]) & (cols[None, :] < col_start[:, None] + kc)
    toep = jnp.where(col_ok, toep, NEG_INF)
    toep = jnp.concatenate([toep, jnp.full((h, 1, w, w), NEG_INF, f32)], axis=1)
    n_cfg, r_tile, krw = rel_all.shape
    b = toep[:, rel_all.reshape(-1)].reshape(h, n_cfg, r_tile, krw, w, w)
    return b.transpose(0, 1, 2, 4, 3, 5).reshape(h, n_cfg, r_tile * w, krw * w)


def _attn_even(p, o_width, lam, lam_init, g_sub, rpb, ck_a, cv_a, ck_b, cv_b, dims, tables):
    bc, seq, bs, s_lat, past = dims
    t = p.shape[0]
    tc = bc * seq
    h_a = ck_a.shape[-1] // HEAD_DIM
    h_b = ck_b.shape[-1] // HEAD_DIM
    cos, sup, sdn = tables
    hd = HEAD_DIM
    lam_arr = jnp.reshape(lam, (1,)).astype(f32)
    g2 = g_sub.reshape(1, hd)
    smem = pl.BlockSpec(memory_space=pltpu.SMEM)
    o_shape = jax.ShapeDtypeStruct((t, o_width), bf16)

    nh = _pick(h_a, CTX_HEADS)
    ga = h_a // nh
    wb = nh * hd
    o = pl.pallas_call(
        functools.partial(_diff_ctx_kernel, nh=nh, lam_init=lam_init),
        grid=(bc, ga),
        in_specs=[smem,
                  pl.BlockSpec((seq, wb), lambda b, h: (b, h)),
                  pl.BlockSpec((seq, wb), lambda b, h: (b, ga + h)),
                  pl.BlockSpec((seq, wb), lambda b, h: (b, 2 * ga + h)),
                  pl.BlockSpec((1, hd), lambda b, h: (0, 0))],
        out_specs=pl.BlockSpec((seq, wb), lambda b, h: (b, h)),
        out_shape=o_shape,
        compiler_params=_cp("parallel", "parallel"),
        name="attn_diff_ctx",
    )(lam_arr, p, p, p, g2)

    nhb = _pick(h_b, CTX_HEADS)
    gb = h_b // nhb
    wbb = nhb * hd
    base = 3 * h_a * hd // wbb
    o = pl.pallas_call(
        lambda o_in, q, k, v, o_out: _soft_ctx_kernel(q, k, v, o_out, nh=nhb),
        grid=(bc, gb),
        in_specs=[pl.BlockSpec(memory_space=pl.ANY),
                  pl.BlockSpec((seq, wbb), lambda b, h: (b, base + h)),
                  pl.BlockSpec((seq, wbb), lambda b, h: (b, base + gb + h)),
                  pl.BlockSpec((seq, wbb), lambda b, h: (b, base + 2 * gb + h))],
        out_specs=pl.BlockSpec((seq, wbb), lambda b, h: (b, h_a * hd // wbb + h)),
        out_shape=o_shape,
        input_output_aliases={0: 0},
        compiler_params=_cp("parallel", "parallel"),
        name="attn_soft_ctx",
    )(o, p, p, p)

    tq = _pick(s_lat, Q_TILE)
    nq = s_lat // tq
    full = lambda b, h, i: (0, 0)
    o = pl.pallas_call(
        lambda o_in, *refs: _diff_lat_kernel(*refs, tq=tq, s_self=s_lat, s_cache=past, nq=nq, lam_init=lam_init),
        grid=(bs, h_a, nq),
        in_specs=[pl.BlockSpec(memory_space=pl.ANY), smem,
                  pl.BlockSpec((tq, hd), lambda b, h, i: (tc // tq + b * nq + i, h)),
                  pl.BlockSpec((tq, hd), lambda b, h, i: (tc // tq + b * nq + jnp.minimum(i + 1, nq - 1), h)),
                  pl.BlockSpec((s_lat, hd), lambda b, h, i: (tc // s_lat + b, h_a + h)),
                  pl.BlockSpec((s_lat, hd), lambda b, h, i: (tc // s_lat + b, 2 * h_a + h)),
                  pl.BlockSpec((None, past, hd), lambda b, h, i: (b, 0, h)),
                  pl.BlockSpec((None, past, hd), lambda b, h, i: (b, 0, h)),
                  pl.BlockSpec((s_lat, hd), full), pl.BlockSpec((s_lat, hd), full), pl.BlockSpec((s_lat, hd), full),
                  pl.BlockSpec((1, hd), full)],
        out_specs=pl.BlockSpec((tq, hd), lambda b, h, i: (tc // tq + b * nq + i, h)),
        out_shape=o_shape,
        scratch_shapes=[pltpu.VMEM((s_lat + past, hd), bf16), pltpu.VMEM((s_lat + past, 2 * hd), bf16),
                        pltpu.VMEM((2, tq, s_lat + past), f32), pltpu.VMEM((2, tq, s_lat + past), f32),
                        pltpu.VMEM((2, tq, 1), f32), pltpu.VMEM((2, tq, 1), f32)],
        input_output_aliases={0: 0},
        compiler_params=_cp("parallel", "parallel", "arbitrary"),
        name="attn_diff_lat",
    )(o, lam_arr, p, p, p, p, ck_a, cv_a, cos, sup, sdn, g2)

    rows = s_lat // GRID_W
    r_tile, krw, ks_np, cfg_np, rel_all = _nbr_tables(rows)
    tqn, kw = r_tile * GRID_W, krw * GRID_W
    nqn = rows // r_tile
    bias = _nbr_bias(rpb, rel_all)
    nhn = _pick(math.gcd(h_a, h_b), NBR_HEADS)
    wn = nhn * hd
    gn, ga_n = h_b // nhn, h_a // nhn
    o = pl.pallas_call(
        lambda ks, cfg, o_in, *refs: _nbr_kernel(ks, cfg, *refs, kw=kw, nh=nhn),
        grid_spec=pltpu.PrefetchScalarGridSpec(
            num_scalar_prefetch=2,
            grid=(bs, gn, nqn),
            in_specs=[pl.BlockSpec(memory_space=pl.ANY),
                      pl.BlockSpec((tqn, wn), lambda b, h, i, ks, cfg: (tc // tqn + b * nqn + i, 3 * ga_n + h)),
                      pl.BlockSpec((s_lat, wn), lambda b, h, i, ks, cfg: (tc // s_lat + b, 3 * ga_n + gn + h)),
                      pl.BlockSpec((s_lat, wn), lambda b, h, i, ks, cfg: (tc // s_lat + b, 3 * ga_n + 2 * gn + h)),
                      pl.BlockSpec((None, past, wn), lambda b, h, i, ks, cfg: (b, 0, h)),
                      pl.BlockSpec((None, past, wn), lambda b, h, i, ks, cfg: (b, 0, h)),
                      pl.BlockSpec((nhn, None, tqn, kw), lambda b, h, i, ks, cfg: (h, cfg[i], 0, 0))],
            out_specs=pl.BlockSpec((tqn, wn), lambda b, h, i, ks, cfg: (tc // tqn + b * nqn + i, ga_n + h)),
        ),
        out_shape=o_shape,
        input_output_aliases={2: 0},
        compiler_params=_cp("parallel", "parallel", "arbitrary"),
        name="attn_nbr_lat",
    )(jnp.asarray(ks_np), jnp.asarray(cfg_np), o, p, p, p, ck_b, cv_b, bias)
    return o


def _attn_mla(q, kv, krd, krd_cache, h_c, dims, tables):
    bc, seq, bs, s_lat, past = dims
    t = q.shape[0]
    tc = bc * seq
    cos, sup, sdn = tables
    hp = h_c // 2
    w2 = 2 * NOPE_DIM
    o_shape = jax.ShapeDtypeStruct((t, h_c * V_DIM), bf16)

    npair = _pick(hp, CTX_HEADS // 2)
    gp = hp // npair
    wn, wr = npair * w2, npair * LANES
    o = pl.pallas_call(
        functools.partial(_mla_ctx_kernel, npair=npair),
        grid=(bc, gp),
        in_specs=[pl.BlockSpec((seq, wn), lambda b, h: (b, h)),
                  pl.BlockSpec((seq, wr), lambda b, h: (b, 2 * gp + h)),
                  pl.BlockSpec((seq, wn), lambda b, h: (b, h)),
                  pl.BlockSpec((seq, wn), lambda b, h: (b, gp + h)),
                  pl.BlockSpec((seq, LANES), lambda b, h: (b, 0))],
        out_specs=pl.BlockSpec((seq, wn), lambda b, h: (b, h)),
        out_shape=o_shape,
        compiler_params=_cp("parallel", "parallel"),
        name="attn_mla_ctx",
    )(q, q, kv, kv, krd)

    tq = _pick(s_lat, Q_TILE)
    nq = s_lat // tq
    full = lambda b, h, i: (0, 0)
    o = pl.pallas_call(
        lambda o_in, *refs: _mla_lat_kernel(*refs, tq=tq, s_self=s_lat, s_cache=past, nq=nq),
        grid=(bs, hp, nq),
        in_specs=[pl.BlockSpec(memory_space=pl.ANY),
                  pl.BlockSpec((tq, w2), lambda b, h, i: (tc // tq + b * nq + i, h)),
                  pl.BlockSpec((tq, LANES), lambda b, h, i: (tc // tq + b * nq + i, 2 * hp + h)),
                  pl.BlockSpec((tq, w2), lambda b, h, i: (tc // tq + b * nq + jnp.minimum(i + 1, nq - 1), h)),
                  pl.BlockSpec((tq, LANES),
                               lambda b, h, i: (tc // tq + b * nq + jnp.minimum(i + 1, nq - 1), 2 * hp + h)),
                  pl.BlockSpec((s_lat, w2), lambda b, h, i: (tc // s_lat + b, h)),
                  pl.BlockSpec((s_lat, w2), lambda b, h, i: (tc // s_lat + b, hp + h)),
                  pl.BlockSpec((s_lat, LANES), lambda b, h, i: (tc // s_lat + b, 0)),
                  pl.BlockSpec((past, w2), lambda b, h, i: (t // past + b, h)),
                  pl.BlockSpec((past, w2), lambda b, h, i: (t // past + b, hp + h)),
                  pl.BlockSpec((None, past, LANES), lambda b, h, i: (b, 0, 0)),
                  pl.BlockSpec((s_lat, LANES), full), pl.BlockSpec((s_lat, LANES), full),
                  pl.BlockSpec((s_lat, LANES), full)],
        out_specs=pl.BlockSpec((tq, w2), lambda b, h, i: (tc // tq + b * nq + i, h)),
        out_shape=o_shape,
        scratch_shapes=[pltpu.VMEM((2, s_lat + past, w2), bf16), pltpu.VMEM((2, s_lat + past, 2 * V_DIM), bf16),
                        pltpu.VMEM((2, tq, s_lat + past), f32), pltpu.VMEM((2, tq, s_lat + past), f32),
                        pltpu.VMEM((2, tq, 1), f32), pltpu.VMEM((2, tq, 1), f32)],
        input_output_aliases={0: 0},
        compiler_params=_cp("parallel", "parallel", "arbitrary"),
        name="attn_mla_lat",
    )(o, q, q, q, q, kv, kv, krd, kv, kv, krd_cache, cos, sup, sdn)
    return o


def _route(logits):
    lane = lax.broadcasted_iota(jnp.int32, logits.shape, 1)
    big = jnp.int32(LANES)
    neg = -jnp.inf

    def top(vals):
        v = jnp.max(vals, axis=-1, keepdims=True)
        return v, jnp.min(jnp.where(vals == v, lane, big), axis=-1, keepdims=True)

    gl = jnp.where(lane < N_GROUPS, logits, neg)
    gmax, g_idx = top(gl)
    p_group = 1.0 / jnp.sum(jnp.exp(gl - gmax), axis=-1, keepdims=True)
    lo = N_GROUPS + g_idx * EXPERTS_PER_GROUP
    el = jnp.where((lane >= lo) & (lane < lo + EXPERTS_PER_GROUP), logits, neg)
    v1, i1 = top(el)
    v2, i2 = top(jnp.where(lane == i1, neg, el))
    e = jnp.exp(v2 - v1)
    w1 = p_group / (1.0 + e)
    eid = jnp.where(lane == 0, i1 - N_GROUPS, jnp.where(lane == 1, i2 - N_GROUPS, 0))
    wgt = jnp.where(lane == 0, w1, jnp.where(lane == 1, w1 * e, 0.0))
    cnt = jnp.sum(((lane == i1) | (lane == i2)).astype(jnp.int32), axis=0, keepdims=True)
    return eid, wgt, cnt


def _moe_kernel(te_ref, nu_ref, tok_ref, dst_ref, hp_ref, rw_ref, w1_ref, w3_ref, w2_ref, y_ref,
                xg0, xg1, yb0, yb1, gsem, ssem, *, tm):
    del te_ref
    i = pl.program_id(0)
    n_used = nu_ref[0]
    xg, yb = (xg0, xg1), (yb0, yb1)

    def gather_copy(row, r, slot):
        return pltpu.make_async_copy(hp_ref.at[pl.ds(row, 1)], xg[slot].at[pl.ds(r, 1)], gsem.at[slot])

    def scatter_copy(row, r, slot):
        return pltpu.make_async_copy(yb[slot].at[pl.ds(r, 1)], y_ref.at[pl.ds(row, 1)], ssem.at[slot])

    def gather_wait(slot):
        pltpu.make_async_copy(hp_ref.at[pl.ds(0, tm)], xg[slot], gsem.at[slot]).wait()

    def scatter_wait(slot):
        pltpu.make_async_copy(yb[slot], y_ref.at[pl.ds(0, tm)], ssem.at[slot]).wait()

    def compute(slot):
        u = xg[slot][...]
        half = u.shape[1]
        x_lo = lax.bitcast_convert_type(u << 16, f32).astype(bf16)
        x_hi = lax.bitcast_convert_type(u & jnp.uint32(0xFFFF0000), f32).astype(bf16)
        a = (jnp.dot(x_lo, w1_ref[0:half, :], preferred_element_type=f32)
             + jnp.dot(x_hi, w1_ref[half:2 * half, :], preferred_element_type=f32))
        b = (jnp.dot(x_lo, w3_ref[0:half, :], preferred_element_type=f32)
             + jnp.dot(x_hi, w3_ref[half:2 * half, :], preferred_element_type=f32))
        hid = (jax.nn.silu(a) * b * rw_ref[...]).astype(bf16)
        yb[slot][...] = jnp.dot(hid, w2_ref[...], preferred_element_type=f32)

    @pl.when(i == 0)
    def _():
        yb1[...] = jnp.zeros_like(yb1)

        def prime(r, carry):
            gather_copy(tok_ref[r], r, 0).start()
            return carry

        lax.fori_loop(0, tm, prime, 0)

    for parity in (0, 1):
        cur, nxt = parity, 1 - parity

        @pl.when((i % 2 == parity) & (i >= 1) & (i <= n_used))
        def _():
            scatter_wait(cur)

        @pl.when((i % 2 == parity) & (i < n_used))
        def _():
            gather_wait(cur)
            for r in range(tm):
                gather_copy(tok_ref[(i + 1) * tm + r], r, nxt).start()
            for r in range(tm):
                scatter_copy(dst_ref[i * tm + r], r, nxt).start()

        @pl.when((i % 2 == parity) & (i + 1 <= n_used))
        def _():
            compute(cur)

        @pl.when((i % 2 == parity) & (i == n_used))
        def _():
            gather_wait(cur)

            def flush(r, carry):
                scatter_copy(dst_ref[i * tm + r], r, nxt).start()
                return carry

            lax.fori_loop(0, tm, flush, 0)
            scatter_wait(nxt)


def _moe(hp, eid, wgt, counts, w1, w3, w2, layer):
    t, half = hp.shape
    d = 2 * half
    _, n_exp, _, f = w1.shape
    tm = MOE_TILE
    n_assign = 2 * t
    n_tiles = n_assign // tm + n_exp + 1
    n_rows = n_tiles * tm

    e_flat = eid.T.reshape(-1)
    w_flat = wgt.T.reshape(-1)
    order = jnp.argsort(e_flat, stable=True).astype(jnp.int32)
    padded = ((counts + tm - 1) // tm) * tm
    pad_end = jnp.cumsum(padded)
    pad_start = pad_end - padded
    src_start = jnp.cumsum(counts) - counts
    n_used = (pad_end[-1] // tm).astype(jnp.int32).reshape(1)
    tile_start = jnp.arange(n_tiles, dtype=jnp.int32) * tm
    tile_e = jnp.minimum(jnp.sum((pad_end[None, :] <= tile_start[:, None]).astype(jnp.int32), axis=1), n_exp - 1)
    rows = jnp.arange(n_rows, dtype=jnp.int32)
    row_e = jnp.broadcast_to(tile_e[:, None], (n_tiles, tm)).reshape(-1)
    j = rows - pad_start[row_e]
    valid = (j < counts[row_e]) & (rows < pad_end[-1])
    a = order[jnp.clip(src_start[row_e] + j, 0, n_assign - 1)]
    spare = n_assign + rows % tm
    row_tok = jnp.where(valid, a % t, 0).astype(jnp.int32)
    row_dst = jnp.concatenate([spare[:tm], jnp.where(valid, a, spare)]).astype(jnp.int32)
    row_w = jnp.where(valid, w_flat[a], 0.0).astype(f32).reshape(n_rows, 1)

    return pl.pallas_call(
        functools.partial(_moe_kernel, tm=tm),
        grid_spec=pltpu.PrefetchScalarGridSpec(
            num_scalar_prefetch=4,
            grid=(n_tiles,),
            in_specs=[pl.BlockSpec(memory_space=pl.ANY),
                      pl.BlockSpec((tm, 1), lambda i, te, nu, tok, dst: (i, 0)),
                      pl.BlockSpec((None, None, d, f), lambda i, te, nu, tok, dst: (layer, te[i], 0, 0)),
                      pl.BlockSpec((None, None, d, f), lambda i, te, nu, tok, dst: (layer, te[i], 0, 0)),
                      pl.BlockSpec((None, None, f, d), lambda i, te, nu, tok, dst: (layer, te[i], 0, 0))],
            out_specs=pl.BlockSpec(memory_space=pl.ANY),
            scratch_shapes=[pltpu.VMEM((tm, half), jnp.uint32), pltpu.VMEM((tm, half), jnp.uint32),
                            pltpu.VMEM((tm, d), f32), pltpu.VMEM((tm, d), f32),
                            pltpu.SemaphoreType.DMA((2,)), pltpu.SemaphoreType.DMA((2,))],
        ),
        out_shape=jax.ShapeDtypeStruct((n_assign + tm, d), f32),
        compiler_params=_cp("arbitrary"),
        name="moe",
    )(tile_e, n_used, row_tok, row_dst, hp, row_w, w1, w3, w2)


def kernel(x_prompt, x_sample, c, c_ctx, cache_a_k, cache_a_v, cache_b_k, cache_b_v, cache_c_kv, cache_c_kr,
           g_norm1, g_norm2, g_final, w_ada, b_ada, w_in_ab, w_out_ab, lam_q1, lam_k1, lam_q2, lam_k2,
           g_sub_a, rpb_b, w_down_c, g_q_c, g_kv_c, w_uq_c, w_uk_c, w_uv_c, w_out_c,
           w_group_router, b_group_router, w_expert_router, b_expert_router, w1_moe, w3_moe, w2_moe):
    bc, seq, d = x_prompt.shape
    bs, s_lat, _ = x_sample.shape
    past = cache_a_k.shape[2]
    depth = w_ada.shape[0]
    tc, ts = bc * seq, bs * s_lat
    t = tc + ts
    h_a, h_b = cache_a_k.shape[3], cache_b_k.shape[3]
    ab = (h_a + h_b) * HEAD_DIM
    dims = (bc, seq, bs, s_lat, past)
    assert tc % s_lat == 0 and t % past == 0 and s_lat % (2 * ROW_TILE) == 0

    x = jnp.concatenate([x_prompt.reshape(tc, d), x_sample.reshape(ts, d)], axis=0)
    cond = jnp.concatenate([c_ctx[None], c, jnp.zeros((COND_ROWS_PAD - 1 - bs, d), f32)], axis=0)
    mod = _ada_mod(cond, w_ada, b_ada)
    tables = _rope_tables(s_lat)

    n_exp = N_GROUPS * EXPERTS_PER_GROUP
    f = w1_moe.shape[-1]
    w1_all = w1_moe.reshape(depth, n_exp, d, f).astype(bf16)
    w3_all = w3_moe.reshape(depth, n_exp, d, f).astype(bf16)
    w2_all = w2_moe.reshape(depth, n_exp, f, d).astype(bf16)
    sak, sav, sbk, sbv, sckv, sckr = [], [], [], [], [], []
    moe_pending = None
    for l in range(depth):
        mod4 = mod[l].reshape(COND_ROWS_PAD, 6, 1, d)
        if moe_pending is None:
            h = _norm(x, g_norm1[l], mod4, tc, s_lat, sel=0)[0]
        else:
            x, h = _norm(x, g_norm1[l], mod4, tc, s_lat, sel=0, moe=moe_pending)
        if l % 2 == 0:
            e = l // 2
            lam_init = 0.8 - 0.6 * math.exp(-0.3 * l)
            lam = (jnp.exp(jnp.sum((lam_q1[e] * lam_k1[e]).astype(f32)))
                   - jnp.exp(jnp.sum((lam_q2[e] * lam_k2[e]).astype(f32))) + lam_init)
            p = _matmul(h, w_in_ab[e].astype(bf16), f32)
            wa = h_a * HEAD_DIM
            sak.append(p[:tc, wa:2 * wa].reshape(bc, seq, h_a, HEAD_DIM))
            sav.append(p[:tc, 2 * wa:3 * wa].reshape(bc, seq, h_a, HEAD_DIM))
            wb = h_b * HEAD_DIM
            sbk.append(p[:tc, 3 * wa + wb:3 * wa + 2 * wb].reshape(bc, seq, h_b, HEAD_DIM))
            sbv.append(p[:tc, 3 * wa + 2 * wb:].reshape(bc, seq, h_b, HEAD_DIM))
            o = _attn_even(p, ab, lam, lam_init, g_sub_a[e], rpb_b[e],
                           cache_a_k[:, e].reshape(bs, past, wa), cache_a_v[:, e].reshape(bs, past, wa),
                           cache_b_k[:, e].reshape(bs, past, wb), cache_b_v[:, e].reshape(bs, past, wb),
                           dims, tables)
            w_out = w_out_ab[e].astype(bf16)
        else:
            oi = l // 2
            q_lora, kv_lora = g_q_c.shape[1], g_kv_c.shape[1]
            h_c = w_uk_c.shape[2] // NOPE_DIM
            wd = w_down_c[oi]
            wd = jnp.concatenate([wd, wd[:, q_lora + kv_lora:]], axis=1).astype(bf16)
            cq, ckv, krd = _mla_down(h, wd, g_q_c[oi], g_kv_c[oi])
            sckv.append(ckv[:tc].reshape(bc, seq, kv_lora))
            sckr.append(krd[:tc, :ROPE_DIM].reshape(bc, seq, ROPE_DIM))
            wq = w_uq_c[oi].reshape(q_lora, h_c, NOPE_DIM + ROPE_DIM)
            wq = jnp.concatenate([wq[:, :, :NOPE_DIM].reshape(q_lora, -1), wq[:, :, NOPE_DIM:].reshape(q_lora, -1)],
                                 axis=1).astype(bf16)
            q = _matmul(cq, wq, f32)
            ckv_all = jnp.concatenate([ckv, cache_c_kv[:, oi].reshape(bs * past, kv_lora)], axis=0).astype(bf16)
            wkv = jnp.concatenate([w_uk_c[oi], w_uv_c[oi]], axis=1).astype(bf16)
            kv = _matmul(ckv_all, wkv, bf16)
            krc = cache_c_kr[:, oi]
            o = _attn_mla(q, kv, krd, jnp.concatenate([krc, krc], axis=-1), h_c, dims, tables)
            w_out = w_out_c[oi].astype(bf16)
        x = _matmul(o, w_out, f32, residual=(x, mod4, 2, tc, s_lat))
        w_r = jnp.concatenate([w_group_router[l], w_expert_router[l]], axis=1)
        w_r = jnp.pad(w_r, ((0, 0), (0, LANES - w_r.shape[1]))).astype(bf16)
        b_r = jnp.pad(jnp.concatenate([b_group_router[l], b_expert_router[l]]), (0, LANES - N_GROUPS - n_exp))
        hp, eid, wgt, cnt = _norm(x, g_norm2[l], mod4, tc, s_lat, sel=1,
                                  route=(w_r, b_r.reshape(1, LANES).astype(f32)))
        counts = jnp.sum(cnt, axis=(0, 1))[N_GROUPS:N_GROUPS + n_exp]
        y = _moe(hp, eid[:, :2], wgt[:, :2], counts, w1_all, w3_all, w2_all, l)
        moe_pending = (y, mod4, 5)
    y_prompt = _norm(x, g_final, None, tc, s_lat, moe=moe_pending, final=True, rows=(0, tc))[0].reshape(bc, seq, d)
    y_sample = _norm(x, g_final, None, tc, s_lat, moe=moe_pending, final=True, rows=(tc, ts))[0].reshape(bs, s_lat, d)
    return (y_prompt, y_sample, jnp.stack(sak, axis=1), jnp.stack(sav, axis=1), jnp.stack(sbk, axis=1),
            jnp.stack(sbv, axis=1), jnp.stack(sckv, axis=1), jnp.stack(sckr, axis=1))
```

```python
import functools
import math

import numpy as np
import jax
import jax.numpy as jnp
from jax import lax
from jax.experimental import pallas as pl
from jax.experimental.pallas import tpu as pltpu

GRID_W = 64
HEAD_DIM = 128
WIN_ROWS = 8
WIN_COLS = 16
NOPE_DIM = 128
ROPE_DIM = 64
V_DIM = 128
N_GROUPS = 4
EXPERTS_PER_GROUP = 8
ROPE_THETA = 10000.0
NORM_EPS = 1e-6
NEG_INF = -1e30
LOG2E = math.log2(math.e)

LANES = 128
COND_ROWS_PAD = 16
VMEM_LIMIT = 56 * 1024 * 1024
ROW_TILE = 512
COL_TILE = 512
NORM_TILE = 256
MOE_TILE = 256
Q_TILE = 512
NBR_ROWS = 4
CTX_HEADS = 8
NBR_HEADS = 2

f32 = jnp.float32
bf16 = jnp.bfloat16


def _cp(*sem):
    return pltpu.CompilerParams(dimension_semantics=sem, vmem_limit_bytes=VMEM_LIMIT)


def _pick(n, pref):
    if n <= pref:
        return n
    t = pref
    while n % t:
        t //= 2
    return t


def _cond_row(r0, tc, s_lat):
    return jnp.where(r0 < tc, 0, 1 + (r0 - tc) // s_lat)


def _dot_nt(a, b):
    return lax.dot_general(a, b, (((1,), (1,)), ((), ())), preferred_element_type=f32)


def _ada_kernel(c_ref, w_ref, b_ref, o_ref):
    s = jax.nn.silu(c_ref[...]).astype(bf16)
    o_ref[...] = jnp.dot(s, w_ref[...].astype(bf16), preferred_element_type=f32) + b_ref[...]


def _ada_mod(cond, w_ada, b_ada):
    depth, d, n = w_ada.shape
    r = cond.shape[0]
    tn = _pick(n, COL_TILE)
    return pl.pallas_call(
        _ada_kernel,
        grid=(depth, n // tn),
        in_specs=[pl.BlockSpec((r, d), lambda l, j: (0, 0)),
                  pl.BlockSpec((None, d, tn), lambda l, j: (l, 0, j)),
                  pl.BlockSpec((None, 1, tn), lambda l, j: (l, 0, j))],
        out_specs=pl.BlockSpec((None, r, tn), lambda l, j: (l, 0, j)),
        out_shape=jax.ShapeDtypeStruct((depth, r, n), f32),
        compiler_params=_cp("parallel", "parallel"),
        name="ada_mod",
    )(cond, w_ada, b_ada.reshape(depth, 1, n))


def _norm_kernel(*refs, combine, final, route):
    it = iter(refs)
    x_ref = next(it)
    if combine:
        y0_ref, y1_ref, gate_ref = next(it), next(it), next(it)
    g_ref = next(it)
    if not final:
        shift_ref, scale_ref = next(it), next(it)
    if route:
        wr_ref, br_ref = next(it), next(it)
    x = x_ref[...]
    if combine:
        x = x + gate_ref[...] * (y0_ref[...] + y1_ref[...])
        if not final:
            next(it)[...] = x
    y = x * lax.rsqrt(jnp.mean(x * x, axis=-1, keepdims=True) + NORM_EPS) * g_ref[...]
    if final:
        next(it)[...] = y
        return
    h = (y * (1.0 + scale_ref[...]) + shift_ref[...]).astype(bf16)
    if not route:
        next(it)[...] = h
        return
    bits = lax.bitcast_convert_type(h.astype(f32), jnp.uint32)
    half = bits.shape[1] // 2
    next(it)[...] = (bits[:, half:] & jnp.uint32(0xFFFF0000)) | (bits[:, :half] >> 16)
    eid, wgt, cnt = _route(jnp.dot(h, wr_ref[...], preferred_element_type=f32) + br_ref[...])
    next(it)[...] = eid
    next(it)[...] = wgt
    next(it)[...] = cnt


def _norm(x, g, mod4, tc, s_lat, *, sel=None, moe=None, final=False, route=None, rows=None):
    t, d = x.shape
    tm = NORM_TILE
    nt = t // tm
    combine = moe is not None
    i0, n_steps = (0, nt) if rows is None else (rows[0] // tm, rows[1] // tm)
    assert rows is None or final
    row = lambda i: (i + i0, 0)
    modspec = lambda j: pl.BlockSpec((None, None, 1, d),
                                     lambda i: (_cond_row((i + i0) * tm, tc, s_lat), j, 0, 0))
    args, specs = [x], [pl.BlockSpec((tm, d), row)]
    if combine:
        y, mod4_moe, gate_j = moe
        args += [y, y, mod4_moe]
        specs += [pl.BlockSpec((tm, d), row), pl.BlockSpec((tm, d), lambda i: (i + i0 + nt, 0)), modspec(gate_j)]
    args.append(g.reshape(1, d))
    specs.append(pl.BlockSpec((1, d), lambda i: (0, 0)))
    if not final:
        args += [mod4, mod4]
        specs += [modspec(3 * sel), modspec(3 * sel + 1)]
    if route is not None:
        args += list(route)
        specs += [pl.BlockSpec((d, LANES), lambda i: (0, 0)), pl.BlockSpec((1, LANES), lambda i: (0, 0))]
    shapes, ospecs = [], []
    if combine and not final:
        shapes.append(jax.ShapeDtypeStruct((t, d), f32))
        ospecs.append(pl.BlockSpec((tm, d), row))
    if final:
        shapes.append(jax.ShapeDtypeStruct((n_steps * tm, d), f32))
        ospecs.append(pl.BlockSpec((tm, d), lambda i: (i, 0)))
    elif route is None:
        shapes.append(jax.ShapeDtypeStruct((t, d), bf16))
        ospecs.append(pl.BlockSpec((tm, d), row))
    else:
        shapes += [jax.ShapeDtypeStruct((t, d // 2), jnp.uint32), jax.ShapeDtypeStruct((t, LANES), jnp.int32),
                   jax.ShapeDtypeStruct((t, LANES), f32), jax.ShapeDtypeStruct((nt, 1, LANES), jnp.int32)]
        ospecs += [pl.BlockSpec((tm, d // 2), row), pl.BlockSpec((tm, LANES), row), pl.BlockSpec((tm, LANES), row),
                   pl.BlockSpec((None, 1, LANES), lambda i: (i, 0, 0))]
    return pl.pallas_call(
        functools.partial(_norm_kernel, combine=combine, final=final, route=route is not None),
        grid=(n_steps,),
        in_specs=specs,
        out_specs=ospecs,
        out_shape=shapes,
        input_output_aliases={0: 0} if (combine and not final) else {},
        compiler_params=_cp("parallel"),
        name="norm",
    )(*args)


def _mm_kernel(*refs, residual):
    if residual:
        a_ref, b_ref, x_ref, gate_ref, o_ref = refs
    else:
        a_ref, b_ref, o_ref = refs
    acc = jnp.dot(a_ref[...], b_ref[...], preferred_element_type=f32)
    if residual:
        acc = x_ref[...] + gate_ref[...] * acc
    o_ref[...] = acc.astype(o_ref.dtype)


def _matmul(a, b, out_dtype, *, residual=None):
    m, k = a.shape
    n = b.shape[1]
    tm = _pick(m, 2 * ROW_TILE if (k >= 4 * ROW_TILE or k <= ROW_TILE) else ROW_TILE)
    tn = _pick(n, COL_TILE * max(1, min(4, 4 * ROW_TILE // k)))
    args = [a, b]
    specs = [pl.BlockSpec((tm, k), lambda i, j: (i, 0)), pl.BlockSpec((k, tn), lambda i, j: (0, j))]
    aliases = {}
    if residual is not None:
        x, mod4, gate_j, tc, s_lat = residual
        args += [x, mod4]
        specs += [pl.BlockSpec((tm, tn), lambda i, j: (i, j)),
                  pl.BlockSpec((None, None, 1, tn), lambda i, j: (_cond_row(i * tm, tc, s_lat), gate_j, 0, j))]
        aliases = {2: 0}
    return pl.pallas_call(
        functools.partial(_mm_kernel, residual=residual is not None),
        grid=(m // tm, n // tn),
        in_specs=specs,
        out_specs=pl.BlockSpec((tm, tn), lambda i, j: (i, j)),
        out_shape=jax.ShapeDtypeStruct((m, n), out_dtype),
        input_output_aliases=aliases,
        compiler_params=_cp("parallel", "parallel"),
        name="matmul",
    )(*args)


def _down_kernel(a_ref, w_ref, gq_ref, gkv_ref, cq_ref, ckv_ref, kr_ref, *, q_lora, kv_lora):
    acc = jnp.dot(a_ref[...], w_ref[...], preferred_element_type=f32)
    cq = acc[:, :q_lora]
    cq = cq * lax.rsqrt(jnp.mean(cq * cq, axis=-1, keepdims=True) + NORM_EPS) * gq_ref[...]
    cq_ref[...] = cq.astype(bf16)
    ckv = acc[:, q_lora:q_lora + kv_lora]
    ckv_ref[...] = ckv * lax.rsqrt(jnp.mean(ckv * ckv, axis=-1, keepdims=True) + NORM_EPS) * gkv_ref[...]
    kr_ref[...] = acc[:, q_lora + kv_lora:]


def _mla_down(h, w_down_dup, g_q, g_kv):
    t, d = h.shape
    q_lora, kv_lora = g_q.shape[0], g_kv.shape[0]
    n = w_down_dup.shape[1]
    tm = _pick(t, ROW_TILE)
    return pl.pallas_call(
        functools.partial(_down_kernel, q_lora=q_lora, kv_lora=kv_lora),
        grid=(t // tm,),
        in_specs=[pl.BlockSpec((tm, d), lambda i: (i, 0)),
                  pl.BlockSpec((d, n), lambda i: (0, 0)),
                  pl.BlockSpec((1, q_lora), lambda i: (0, 0)),
                  pl.BlockSpec((1, kv_lora), lambda i: (0, 0))],
        out_specs=[pl.BlockSpec((tm, q_lora), lambda i: (i, 0)),
                   pl.BlockSpec((tm, kv_lora), lambda i: (i, 0)),
                   pl.BlockSpec((tm, LANES), lambda i: (i, 0))],
        out_shape=[jax.ShapeDtypeStruct((t, q_lora), bf16),
                   jax.ShapeDtypeStruct((t, kv_lora), f32),
                   jax.ShapeDtypeStruct((t, LANES), f32)],
        compiler_params=_cp("parallel"),
        name="mla_down",
    )(h, w_down_dup, g_q.reshape(1, -1), g_kv.reshape(1, -1))


def _rope_tables(s_lat):
    half = ROPE_DIM // 2
    quarter = half // 2
    tpos = jnp.arange(s_lat)
    inv = ROPE_THETA ** (-jnp.arange(0, half, 2, dtype=f32) / half)
    ang_r = (tpos // GRID_W).astype(f32)[:, None] * inv
    ang_c = (tpos % GRID_W).astype(f32)[:, None] * inv
    ang = jnp.concatenate([ang_r, ang_r, ang_c, ang_c], axis=-1)
    ang = jnp.tile(ang, (1, LANES // ROPE_DIM))
    first = (jnp.arange(LANES) % half) < quarter
    sin = jnp.sin(ang)
    return jnp.cos(ang), jnp.where(first, -sin, 0.0), jnp.where(first, 0.0, sin)


def _rope(x, cos, sin_up, sin_dn):
    quarter = ROPE_DIM // 4
    return x * cos + pltpu.roll(x, LANES - quarter, 1) * sin_up + pltpu.roll(x, quarter, 1) * sin_dn


def _softmax_parts(s):
    m = jnp.max(s, axis=-1, keepdims=True)
    e = jnp.exp2(s - m)
    return e, jnp.sum(e, axis=-1, keepdims=True)


def _diff_logits(q, kk):
    dqk = HEAD_DIM // 2
    q = q * (dqk ** -0.5 * LOG2E)
    lane = lax.broadcasted_iota(jnp.int32, q.shape, 1)
    return (_dot_nt(jnp.where(lane < dqk, q, 0.0).astype(bf16), kk),
            _dot_nt(jnp.where(lane >= dqk, q, 0.0).astype(bf16), kk))


def _diff_finish(s1, m1, s2, m2, vv, lam, g, lam_init):
    e1, e2 = jnp.exp2(s1 - m1), jnp.exp2(s2 - m2)
    l1, l2 = jnp.sum(e1, axis=-1, keepdims=True), jnp.sum(e2, axis=-1, keepdims=True)
    a = e1 * (1.0 / l1) - e2 * (lam / l2)
    o = jnp.dot(a.astype(bf16), vv, preferred_element_type=f32)
    o = o * lax.rsqrt(jnp.mean(o * o, axis=-1, keepdims=True) + NORM_EPS) * g
    return (o * (1.0 - lam_init)).astype(bf16)


def _diff_finish_aug(s1, m1, s2, m2, vaug, lam, g, lam_init):
    o1 = jnp.dot(jnp.exp2(s1 - m1).astype(bf16), vaug, preferred_element_type=f32)
    o2 = jnp.dot(jnp.exp2(s2 - m2).astype(bf16), vaug, preferred_element_type=f32)
    hd = HEAD_DIM
    o = o1[:, :hd] * (1.0 / o1[:, hd:hd + 1]) - o2[:, :hd] * (lam / o2[:, hd:hd + 1])
    o = o * lax.rsqrt(jnp.mean(o * o, axis=-1, keepdims=True) + NORM_EPS) * g
    return (o * (1.0 - lam_init)).astype(bf16)


def _diff_ctx_kernel(lam_ref, q_ref, k_ref, v_ref, g_ref, o_ref, *, nh, lam_init):
    for hh in range(nh):
        cols = slice(hh * HEAD_DIM, (hh + 1) * HEAD_DIM)
        s1, s2 = _diff_logits(q_ref[:, cols], k_ref[:, cols].astype(bf16))
        o_ref[:, cols] = _diff_finish(s1, jnp.max(s1, axis=-1, keepdims=True), s2, jnp.max(s2, axis=-1, keepdims=True),
                                      v_ref[:, cols].astype(bf16), lam_ref[0], g_ref[...], lam_init)


def _diff_lat_kernel(lam_ref, q_ref, q_nx_ref, k_ref, v_ref, kc_ref, vc_ref, cos_ref, sup_ref, sdn_ref, g_ref, o_ref,
                     kall, vall, s0, s1, m0, m1, *, tq, s_self, s_cache, nq, lam_init):
    qi = pl.program_id(2)
    bufs = ((s0, m0), (s1, m1))

    def logits(q_r, tile, s_buf, m_buf):
        rows = pl.ds(pl.multiple_of(tile * tq, tq), tq)
        q = _rope(q_r[...], cos_ref[rows, :], sup_ref[rows, :], sdn_ref[rows, :])
        for c, s in enumerate(_diff_logits(q, kall[...])):
            s_buf[c] = s
            m_buf[c] = jnp.max(s, axis=-1, keepdims=True)

    @pl.when(qi == 0)
    def _():
        kall[0:s_self, :] = _rope(k_ref[...], cos_ref[...], sup_ref[...], sdn_ref[...]).astype(bf16)
        kall[s_self:s_self + s_cache, :] = kc_ref[...].astype(bf16)
        vall[0:s_self, 0:HEAD_DIM] = v_ref[...].astype(bf16)
        vall[s_self:s_self + s_cache, 0:HEAD_DIM] = vc_ref[...].astype(bf16)
        vall[:, HEAD_DIM:2 * HEAD_DIM] = jnp.ones((s_self + s_cache, HEAD_DIM), bf16)
        logits(q_ref, 0, s0, m0)

    for parity in (0, 1):
        (s_cur, m_cur), (s_nxt, m_nxt) = bufs[parity], bufs[1 - parity]

        @pl.when(qi % 2 == parity)
        def _():
            logits(q_nx_ref, jnp.minimum(qi + 1, nq - 1), s_nxt, m_nxt)
            o_ref[...] = _diff_finish_aug(s_cur[0], m_cur[0], s_cur[1], m_cur[1], vall[...], lam_ref[0],
                                          g_ref[...], lam_init)


def _soft_ctx_kernel(q_ref, k_ref, v_ref, o_ref, *, nh):
    for hh in range(nh):
        cols = slice(hh * HEAD_DIM, (hh + 1) * HEAD_DIM)
        q = (q_ref[:, cols] * (HEAD_DIM ** -0.5 * LOG2E)).astype(bf16)
        e, l = _softmax_parts(_dot_nt(q, k_ref[:, cols].astype(bf16)))
        o = jnp.dot(e.astype(bf16), v_ref[:, cols].astype(bf16), preferred_element_type=f32)
        o_ref[:, cols] = (o * (1.0 / l)).astype(bf16)


def _nbr_kernel(ks_ref, cfg_ref, q_ref, k_ref, v_ref, kc_ref, vc_ref, bias_ref, o_ref, *, kw, nh):
    del cfg_ref
    t = pl.program_id(2)
    rows = pl.ds(pl.multiple_of(ks_ref[t] * GRID_W, GRID_W), kw)
    for hh in range(nh):
        c = slice(hh * HEAD_DIM, (hh + 1) * HEAD_DIM)
        q = (q_ref[:, c] * (HEAD_DIM ** -0.5 * LOG2E)).astype(bf16)
        s_loc = _dot_nt(q, k_ref[rows, c].astype(bf16)) + bias_ref[hh]
        s_ctx = _dot_nt(q, kc_ref[:, c].astype(bf16))
        m = jnp.maximum(jnp.max(s_loc, axis=-1, keepdims=True), jnp.max(s_ctx, axis=-1, keepdims=True))
        e_loc, e_ctx = jnp.exp2(s_loc - m), jnp.exp2(s_ctx - m)
        l = jnp.sum(e_loc, axis=-1, keepdims=True) + jnp.sum(e_ctx, axis=-1, keepdims=True)
        o = (jnp.dot(e_loc.astype(bf16), v_ref[rows, c].astype(bf16), preferred_element_type=f32)
             + jnp.dot(e_ctx.astype(bf16), vc_ref[:, c].astype(bf16), preferred_element_type=f32))
        o_ref[:, c] = (o * (1.0 / l)).astype(bf16)


def _mla_head(qn, qr_masked, kcat, vv):
    qc = (jnp.concatenate([qn, qr_masked], axis=-1) * ((NOPE_DIM + ROPE_DIM) ** -0.5 * LOG2E)).astype(bf16)
    e, l = _softmax_parts(_dot_nt(qc, kcat))
    o = jnp.dot(e.astype(bf16), vv, preferred_element_type=f32)
    return (o * (1.0 / l)).astype(bf16)


def _mla_ctx_kernel(qn_ref, qr_ref, kn_ref, v_ref, kr_ref, o_ref, *, npair):
    hd = NOPE_DIM
    kr = kr_ref[...].astype(bf16)
    lane = lax.broadcasted_iota(jnp.int32, (qr_ref.shape[0], LANES), 1)
    for pp in range(npair):
        qr = qr_ref[:, pp * LANES:(pp + 1) * LANES]
        for hh in range(2):
            c = slice((2 * pp + hh) * hd, (2 * pp + hh + 1) * hd)
            mask = (lane < ROPE_DIM) if hh == 0 else (lane >= ROPE_DIM)
            kcat = jnp.concatenate([kn_ref[:, c], kr], axis=-1)
            o_ref[:, c] = _mla_head(qn_ref[:, c], jnp.where(mask, qr, 0.0), kcat, v_ref[:, c])


def _mla_lat_kernel(qn_ref, qr_ref, qn_nx_ref, qr_nx_ref, kn_ref, v_ref, kr_ref, knc_ref, vc_ref, krc_ref,
                    cos_ref, sup_ref, sdn_ref, o_ref, kcat, vaug, s0, s1, m0, m1, *, tq, s_self, s_cache, nq):
    qi = pl.program_id(2)
    hd = NOPE_DIM
    s_all = s_self + s_cache
    bufs = ((s0, m0), (s1, m1))

    def logits(qn_r, qr_r, tile, s_buf, m_buf):
        rows = pl.ds(pl.multiple_of(tile * tq, tq), tq)
        qr = _rope(qr_r[...], cos_ref[rows, :], sup_ref[rows, :], sdn_ref[rows, :])
        lane = lax.broadcasted_iota(jnp.int32, qr.shape, 1)
        for hh in range(2):
            mask = (lane < ROPE_DIM) if hh == 0 else (lane >= ROPE_DIM)
            qc = jnp.concatenate([qn_r[:, hh * hd:(hh + 1) * hd], jnp.where(mask, qr, 0.0)], axis=-1)
            s = _dot_nt((qc * ((NOPE_DIM + ROPE_DIM) ** -0.5 * LOG2E)).astype(bf16), kcat[hh])
            s_buf[hh] = s
            m_buf[hh] = jnp.max(s, axis=-1, keepdims=True)

    @pl.when(qi == 0)
    def _():
        kr = _rope(kr_ref[...], cos_ref[...], sup_ref[...], sdn_ref[...]).astype(bf16)
        krc = krc_ref[...].astype(bf16)
        ones = jnp.ones((s_all, V_DIM), bf16)
        for hh in range(2):
            c = slice(hh * hd, (hh + 1) * hd)
            kcat[hh, 0:s_self, 0:hd] = kn_ref[:, c]
            kcat[hh, 0:s_self, hd:2 * hd] = kr
            kcat[hh, s_self:s_all, 0:hd] = knc_ref[:, c]
            kcat[hh, s_self:s_all, hd:2 * hd] = krc
            vaug[hh, 0:s_self, 0:V_DIM] = v_ref[:, c]
            vaug[hh, s_self:s_all, 0:V_DIM] = vc_ref[:, c]
            vaug[hh, :, V_DIM:2 * V_DIM] = ones
        logits(qn_ref, qr_ref, 0, s0, m0)

    for parity in (0, 1):
        (s_cur, m_cur), (s_nxt, m_nxt) = bufs[parity], bufs[1 - parity]

        @pl.when(qi % 2 == parity)
        def _():
            logits(qn_nx_ref, qr_nx_ref, jnp.minimum(qi + 1, nq - 1), s_nxt, m_nxt)
            for hh in range(2):
                e = jnp.exp2(s_cur[hh] - m_cur[hh]).astype(bf16)
                oa = jnp.dot(e, vaug[hh], preferred_element_type=f32)
                o_ref[:, hh * V_DIM:(hh + 1) * V_DIM] = (oa[:, :V_DIM] * (1.0 / oa[:, V_DIM:V_DIM + 1])).astype(bf16)


def _nbr_tables(rows):
    r_tile = min(NBR_ROWS, rows)
    kr = min(WIN_ROWS, rows)
    krw = min(rows, r_tile + kr)
    n_rel = 2 * WIN_ROWS - 1
    ks_list, cfg_list, cfgs = [], [], {}
    for t in range(rows // r_tile):
        qr = t * r_tile + np.arange(r_tile)
        r0 = np.clip(qr - kr // 2, 0, rows - kr)
        ks = int(np.clip(t * r_tile - kr // 2, 0, rows - krw))
        krow = ks + np.arange(krw)
        row_ok = (krow[None, :] >= r0[:, None]) & (krow[None, :] < r0[:, None] + kr)
        rel = np.where(row_ok, krow[None, :] - qr[:, None] + WIN_ROWS - 1, n_rel).astype(np.int32)
        key = rel.tobytes()
        if key not in cfgs:
            cfgs[key] = (len(cfgs), rel)
        ks_list.append(ks)
        cfg_list.append(cfgs[key][0])
    rel_all = np.stack([c[1] for c in sorted(cfgs.values(), key=lambda c: c[0])])
    return r_tile, krw, np.asarray(ks_list, np.int32), np.asarray(cfg_list, np.int32), rel_all


def _nbr_bias(rpb, rel_all):
    w = GRID_W
    kc = min(WIN_COLS, w)
    h, n_rel, _ = rpb.shape
    span = 2 * w - 1
    off = np.clip(np.arange(span) - (w - 1), -(WIN_COLS - 1), WIN_COLS - 1) + WIN_COLS - 1
    v = rpb[:, :, off].astype(f32) * LOG2E
    m = jnp.tile(v, (1, 1, w + 1))[:, :, :w * (span + 1)].reshape(h, n_rel, w, span + 1)[:, :, :, :w]
    toep = m[:, :, ::-1, :]
    cols = np.arange(w)
    col_start = np.clip(cols - kc // 2, 0, w - kc)
    col_ok = (cols[None, :] >= col_start[:, None]) & (cols[None, :] < col_start[:, None] + kc)
    toep = jnp.where(col_ok, toep, NEG_INF)
    toep = jnp.concatenate([toep, jnp.full((h, 1, w, w), NEG_INF, f32)], axis=1)
    n_cfg, r_tile, krw = rel_all.shape
    b = toep[:, rel_all.reshape(-1)].reshape(h, n_cfg, r_tile, krw, w, w)
    return b.transpose(0, 1, 2, 4, 3, 5).reshape(h, n_cfg, r_tile * w, krw * w)


def _attn_even(p, o_width, lam, lam_init, g_sub, rpb, ck_a, cv_a, ck_b, cv_b, dims, tables):
    bc, seq, bs, s_lat, past = dims
    t = p.shape[0]
    tc = bc * seq
    h_a = ck_a.shape[-1] // HEAD_DIM
    h_b = ck_b.shape[-1] // HEAD_DIM
    cos, sup, sdn = tables
    hd = HEAD_DIM
    lam_arr = jnp.reshape(lam, (1,)).astype(f32)
    g2 = g_sub.reshape(1, hd)
    smem = pl.BlockSpec(memory_space=pltpu.SMEM)
    o_shape = jax.ShapeDtypeStruct((t, o_width), bf16)

    nh = _pick(h_a, CTX_HEADS)
    ga = h_a // nh
    wb = nh * hd
    o = pl.pallas_call(
        functools.partial(_diff_ctx_kernel, nh=nh, lam_init=lam_init),
        grid=(bc, ga),
        in_specs=[smem,
                  pl.BlockSpec((seq, wb), lambda b, h: (b, h)),
                  pl.BlockSpec((seq, wb), lambda b, h: (b, ga + h)),
                  pl.BlockSpec((seq, wb), lambda b, h: (b, 2 * ga + h)),
                  pl.BlockSpec((1, hd), lambda b, h: (0, 0))],
        out_specs=pl.BlockSpec((seq, wb), lambda b, h: (b, h)),
        out_shape=o_shape,
        compiler_params=_cp("parallel", "parallel"),
        name="attn_diff_ctx",
    )(lam_arr, p, p, p, g2)

    nhb = _pick(h_b, CTX_HEADS)
    gb = h_b // nhb
    wbb = nhb * hd
    base = 3 * h_a * hd // wbb
    o = pl.pallas_call(
        lambda o_in, q, k, v, o_out: _soft_ctx_kernel(q, k, v, o_out, nh=nhb),
        grid=(bc, gb),
        in_specs=[pl.BlockSpec(memory_space=pl.ANY),
                  pl.BlockSpec((seq, wbb), lambda b, h: (b, base + h)),
                  pl.BlockSpec((seq, wbb), lambda b, h: (b, base + gb + h)),
                  pl.BlockSpec((seq, wbb), lambda b, h: (b, base + 2 * gb + h))],
        out_specs=pl.BlockSpec((seq, wbb), lambda b, h: (b, h_a * hd // wbb + h)),
        out_shape=o_shape,
        input_output_aliases={0: 0},
        compiler_params=_cp("parallel", "parallel"),
        name="attn_soft_ctx",
    )(o, p, p, p)

    tq = _pick(s_lat, Q_TILE)
    nq = s_lat // tq
    full = lambda b, h, i: (0, 0)
    o = pl.pallas_call(
        lambda o_in, *refs: _diff_lat_kernel(*refs, tq=tq, s_self=s_lat, s_cache=past, nq=nq, lam_init=lam_init),
        grid=(bs, h_a, nq),
        in_specs=[pl.BlockSpec(memory_space=pl.ANY), smem,
                  pl.BlockSpec((tq, hd), lambda b, h, i: (tc // tq + b * nq + i, h)),
                  pl.BlockSpec((tq, hd), lambda b, h, i: (tc // tq + b * nq + jnp.minimum(i + 1, nq - 1), h)),
                  pl.BlockSpec((s_lat, hd), lambda b, h, i: (tc // s_lat + b, h_a + h)),
                  pl.BlockSpec((s_lat, hd), lambda b, h, i: (tc // s_lat + b, 2 * h_a + h)),
                  pl.BlockSpec((None, past, hd), lambda b, h, i: (b, 0, h)),
                  pl.BlockSpec((None, past, hd), lambda b, h, i: (b, 0, h)),
                  pl.BlockSpec((s_lat, hd), full), pl.BlockSpec((s_lat, hd), full), pl.BlockSpec((s_lat, hd), full),
                  pl.BlockSpec((1, hd), full)],
        out_specs=pl.BlockSpec((tq, hd), lambda b, h, i: (tc // tq + b * nq + i, h)),
        out_shape=o_shape,
        scratch_shapes=[pltpu.VMEM((s_lat + past, hd), bf16), pltpu.VMEM((s_lat + past, 2 * hd), bf16),
                        pltpu.VMEM((2, tq, s_lat + past), f32), pltpu.VMEM((2, tq, s_lat + past), f32),
                        pltpu.VMEM((2, tq, 1), f32), pltpu.VMEM((2, tq, 1), f32)],
        input_output_aliases={0: 0},
        compiler_params=_cp("parallel", "parallel", "arbitrary"),
        name="attn_diff_lat",
    )(o, lam_arr, p, p, p, p, ck_a, cv_a, cos, sup, sdn, g2)

    rows = s_lat // GRID_W
    r_tile, krw, ks_np, cfg_np, rel_all = _nbr_tables(rows)
    tqn, kw = r_tile * GRID_W, krw * GRID_W
    nqn = rows // r_tile
    bias = _nbr_bias(rpb, rel_all)
    nhn = _pick(math.gcd(h_a, h_b), NBR_HEADS)
    wn = nhn * hd
    gn, ga_n = h_b // nhn, h_a // nhn
    o = pl.pallas_call(
        lambda ks, cfg, o_in, *refs: _nbr_kernel(ks, cfg, *refs, kw=kw, nh=nhn),
        grid_spec=pltpu.PrefetchScalarGridSpec(
            num_scalar_prefetch=2,
            grid=(bs, gn, nqn),
            in_specs=[pl.BlockSpec(memory_space=pl.ANY),
                      pl.BlockSpec((tqn, wn), lambda b, h, i, ks, cfg: (tc // tqn + b * nqn + i, 3 * ga_n + h)),
                      pl.BlockSpec((s_lat, wn), lambda b, h, i, ks, cfg: (tc // s_lat + b, 3 * ga_n + gn + h)),
                      pl.BlockSpec((s_lat, wn), lambda b, h, i, ks, cfg: (tc // s_lat + b, 3 * ga_n + 2 * gn + h)),
                      pl.BlockSpec((None, past, wn), lambda b, h, i, ks, cfg: (b, 0, h)),
                      pl.BlockSpec((None, past, wn), lambda b, h, i, ks, cfg: (b, 0, h)),
                      pl.BlockSpec((nhn, None, tqn, kw), lambda b, h, i, ks, cfg: (h, cfg[i], 0, 0))],
            out_specs=pl.BlockSpec((tqn, wn), lambda b, h, i, ks, cfg: (tc // tqn + b * nqn + i, ga_n + h)),
        ),
        out_shape=o_shape,
        input_output_aliases={2: 0},
        compiler_params=_cp("parallel", "parallel", "arbitrary"),
        name="attn_nbr_lat",
    )(jnp.asarray(ks_np), jnp.asarray(cfg_np), o, p, p, p, ck_b, cv_b, bias)
    return o


def _attn_mla(q, kv, krd, krd_cache, h_c, dims, tables):
    bc, seq, bs, s_lat, past = dims
    t = q.shape[0]
    tc = bc * seq
    cos, sup, sdn = tables
    hp = h_c // 2
    w2 = 2 * NOPE_DIM
    o_shape = jax.ShapeDtypeStruct((t, h_c * V_DIM), bf16)

    npair = _pick(hp, CTX_HEADS // 2)
    gp = hp // npair
    wn, wr = npair * w2, npair * LANES
    o = pl.pallas_call(
        functools.partial(_mla_ctx_kernel, npair=npair),
        grid=(bc, gp),
        in_specs=[pl.BlockSpec((seq, wn), lambda b, h: (b, h)),
                  pl.BlockSpec((seq, wr), lambda b, h: (b, 2 * gp + h)),
                  pl.BlockSpec((seq, wn), lambda b, h: (b, h)),
                  pl.BlockSpec((seq, wn), lambda b, h: (b, gp + h)),
                  pl.BlockSpec((seq, LANES), lambda b, h: (b, 0))],
        out_specs=pl.BlockSpec((seq, wn), lambda b, h: (b, h)),
        out_shape=o_shape,
        compiler_params=_cp("parallel", "parallel"),
        name="attn_mla_ctx",
    )(q, q, kv, kv, krd)

    tq = _pick(s_lat, Q_TILE)
    nq = s_lat // tq
    full = lambda b, h, i: (0, 0)
    o = pl.pallas_call(
        lambda o_in, *refs: _mla_lat_kernel(*refs, tq=tq, s_self=s_lat, s_cache=past, nq=nq),
        grid=(bs, hp, nq),
        in_specs=[pl.BlockSpec(memory_space=pl.ANY),
                  pl.BlockSpec((tq, w2), lambda b, h, i: (tc // tq + b * nq + i, h)),
                  pl.BlockSpec((tq, LANES), lambda b, h, i: (tc // tq + b * nq + i, 2 * hp + h)),
                  pl.BlockSpec((tq, w2), lambda b, h, i: (tc // tq + b * nq + jnp.minimum(i + 1, nq - 1), h)),
                  pl.BlockSpec((tq, LANES),
                               lambda b, h, i: (tc // tq + b * nq + jnp.minimum(i + 1, nq - 1), 2 * hp + h)),
                  pl.BlockSpec((s_lat, w2), lambda b, h, i: (tc // s_lat + b, h)),
                  pl.BlockSpec((s_lat, w2), lambda b, h, i: (tc // s_lat + b, hp + h)),
                  pl.BlockSpec((s_lat, LANES), lambda b, h, i: (tc // s_lat + b, 0)),
                  pl.BlockSpec((past, w2), lambda b, h, i: (t // past + b, h)),
                  pl.BlockSpec((past, w2), lambda b, h, i: (t // past + b, hp + h)),
                  pl.BlockSpec((None, past, LANES), lambda b, h, i: (b, 0, 0)),
                  pl.BlockSpec((s_lat, LANES), full), pl.BlockSpec((s_lat, LANES), full),
                  pl.BlockSpec((s_lat, LANES), full)],
        out_specs=pl.BlockSpec((tq, w2), lambda b, h, i: (tc // tq + b * nq + i, h)),
        out_shape=o_shape,
        scratch_shapes=[pltpu.VMEM((2, s_lat + past, w2), bf16), pltpu.VMEM((2, s_lat + past, 2 * V_DIM), bf16),
                        pltpu.VMEM((2, tq, s_lat + past), f32), pltpu.VMEM((2, tq, s_lat + past), f32),
                        pltpu.VMEM((2, tq, 1), f32), pltpu.VMEM((2, tq, 1), f32)],
        input_output_aliases={0: 0},
        compiler_params=_cp("parallel", "parallel", "arbitrary"),
        name="attn_mla_lat",
    )(o, q, q, q, q, kv, kv, krd, kv, kv, krd_cache, cos, sup, sdn)
    return o


def _route(logits):
    lane = lax.broadcasted_iota(jnp.int32, logits.shape, 1)
    big = jnp.int32(LANES)
    neg = -jnp.inf

    def top(vals):
        v = jnp.max(vals, axis=-1, keepdims=True)
        return v, jnp.min(jnp.where(vals == v, lane, big), axis=-1, keepdims=True)

    gl = jnp.where(lane < N_GROUPS, logits, neg)
    gmax, g_idx = top(gl)
    p_group = 1.0 / jnp.sum(jnp.exp(gl - gmax), axis=-1, keepdims=True)
    lo = N_GROUPS + g_idx * EXPERTS_PER_GROUP
    el = jnp.where((lane >= lo) & (lane < lo + EXPERTS_PER_GROUP), logits, neg)
    v1, i1 = top(el)
    v2, i2 = top(jnp.where(lane == i1, neg, el))
    e = jnp.exp(v2 - v1)
    w1 = p_group / (1.0 + e)
    eid = jnp.where(lane == 0, i1 - N_GROUPS, jnp.where(lane == 1, i2 - N_GROUPS, 0))
    wgt = jnp.where(lane == 0, w1, jnp.where(lane == 1, w1 * e, 0.0))
    cnt = jnp.sum(((lane == i1) | (lane == i2)).astype(jnp.int32), axis=0, keepdims=True)
    return eid, wgt, cnt


def _moe_kernel(te_ref, nu_ref, tok_ref, dst_ref, hp_ref, rw_ref, w1_ref, w3_ref, w2_ref, y_ref,
                xg0, xg1, yb0, yb1, gsem, ssem, *, tm):
    del te_ref
    i = pl.program_id(0)
    n_used = nu_ref[0]
    xg, yb = (xg0, xg1), (yb0, yb1)

    def gather_copy(row, r, slot):
        return pltpu.make_async_copy(hp_ref.at[pl.ds(row, 1)], xg[slot].at[pl.ds(r, 1)], gsem.at[slot])

    def scatter_copy(row, r, slot):
        return pltpu.make_async_copy(yb[slot].at[pl.ds(r, 1)], y_ref.at[pl.ds(row, 1)], ssem.at[slot])

    def gather_wait(slot):
        pltpu.make_async_copy(hp_ref.at[pl.ds(0, tm)], xg[slot], gsem.at[slot]).wait()

    def scatter_wait(slot):
        pltpu.make_async_copy(yb[slot], y_ref.at[pl.ds(0, tm)], ssem.at[slot]).wait()

    def compute(slot):
        u = xg[slot][...]
        half = u.shape[1]
        x_lo = lax.bitcast_convert_type(u << 16, f32).astype(bf16)
        x_hi = lax.bitcast_convert_type(u & jnp.uint32(0xFFFF0000), f32).astype(bf16)
        a = (jnp.dot(x_lo, w1_ref[0:half, :], preferred_element_type=f32)
             + jnp.dot(x_hi, w1_ref[half:2 * half, :], preferred_element_type=f32))
        b = (jnp.dot(x_lo, w3_ref[0:half, :], preferred_element_type=f32)
             + jnp.dot(x_hi, w3_ref[half:2 * half, :], preferred_element_type=f32))
        hid = (jax.nn.silu(a) * b * rw_ref[...]).astype(bf16)
        yb[slot][...] = jnp.dot(hid, w2_ref[...], preferred_element_type=f32)

    @pl.when(i == 0)
    def _():
        yb1[...] = jnp.zeros_like(yb1)

        def prime(r, carry):
            gather_copy(tok_ref[r], r, 0).start()
            return carry

        lax.fori_loop(0, tm, prime, 0)

    for parity in (0, 1):
        cur, nxt = parity, 1 - parity

        @pl.when((i % 2 == parity) & (i >= 1) & (i <= n_used))
        def _():
            scatter_wait(cur)

        @pl.when((i % 2 == parity) & (i < n_used))
        def _():
            gather_wait(cur)
            for r in range(tm):
                gather_copy(tok_ref[(i + 1) * tm + r], r, nxt).start()
            for r in range(tm):
                scatter_copy(dst_ref[i * tm + r], r, nxt).start()

        @pl.when((i % 2 == parity) & (i + 1 <= n_used))
        def _():
            compute(cur)

        @pl.when((i % 2 == parity) & (i == n_used))
        def _():
            gather_wait(cur)

            def flush(r, carry):
                scatter_copy(dst_ref[i * tm + r], r, nxt).start()
                return carry

            lax.fori_loop(0, tm, flush, 0)
            scatter_wait(nxt)


def _moe(hp, eid, wgt, counts, w1, w3, w2, layer):
    t, half = hp.shape
    d = 2 * half
    _, n_exp, _, f = w1.shape
    tm = MOE_TILE
    n_assign = 2 * t
    n_tiles = n_assign // tm + n_exp + 1
    n_rows = n_tiles * tm

    e_flat = eid.T.reshape(-1)
    w_flat = wgt.T.reshape(-1)
    order = jnp.argsort(e_flat, stable=True).astype(jnp.int32)
    padded = ((counts + tm - 1) // tm) * tm
    pad_end = jnp.cumsum(padded)
    pad_start = pad_end - padded
    src_start = jnp.cumsum(counts) - counts
    n_used = (pad_end[-1] // tm).astype(jnp.int32).reshape(1)
    tile_start = jnp.arange(n_tiles, dtype=jnp.int32) * tm
    tile_e = jnp.minimum(jnp.sum((pad_end[None, :] <= tile_start[:, None]).astype(jnp.int32), axis=1), n_exp - 1)
    rows = jnp.arange(n_rows, dtype=jnp.int32)
    j = (rows.reshape(n_tiles, tm) - pad_start[tile_e][:, None]).reshape(-1)
    n_valid = jnp.broadcast_to(counts[tile_e][:, None], (n_tiles, tm)).reshape(-1)
    src0 = jnp.broadcast_to(src_start[tile_e][:, None], (n_tiles, tm)).reshape(-1)
    valid = (j < n_valid) & (rows < pad_end[-1])
    a = order[jnp.clip(src0 + j, 0, n_assign - 1)]
    spare = n_assign + rows % tm
    row_tok = jnp.where(valid, a % t, 0).astype(jnp.int32)
    row_dst = jnp.concatenate([spare[:tm], jnp.where(valid, a, spare)]).astype(jnp.int32)
    row_w = jnp.where(valid, w_flat[a], 0.0).astype(f32).reshape(n_rows, 1)

    return pl.pallas_call(
        functools.partial(_moe_kernel, tm=tm),
        grid_spec=pltpu.PrefetchScalarGridSpec(
            num_scalar_prefetch=4,
            grid=(n_tiles,),
            in_specs=[pl.BlockSpec(memory_space=pl.ANY),
                      pl.BlockSpec((tm, 1), lambda i, te, nu, tok, dst: (i, 0)),
                      pl.BlockSpec((None, None, d, f), lambda i, te, nu, tok, dst: (layer, te[i], 0, 0)),
                      pl.BlockSpec((None, None, d, f), lambda i, te, nu, tok, dst: (layer, te[i], 0, 0)),
                      pl.BlockSpec((None, None, f, d), lambda i, te, nu, tok, dst: (layer, te[i], 0, 0))],
            out_specs=pl.BlockSpec(memory_space=pl.ANY),
            scratch_shapes=[pltpu.VMEM((tm, half), jnp.uint32), pltpu.VMEM((tm, half), jnp.uint32),
                            pltpu.VMEM((tm, d), f32), pltpu.VMEM((tm, d), f32),
                            pltpu.SemaphoreType.DMA((2,)), pltpu.SemaphoreType.DMA((2,))],
        ),
        out_shape=jax.ShapeDtypeStruct((n_assign + tm, d), f32),
        compiler_params=_cp("arbitrary"),
        name="moe",
    )(tile_e, n_used, row_tok, row_dst, hp, row_w, w1, w3, w2)


def kernel(x_prompt, x_sample, c, c_ctx, cache_a_k, cache_a_v, cache_b_k, cache_b_v, cache_c_kv, cache_c_kr,
           g_norm1, g_norm2, g_final, w_ada, b_ada, w_in_ab, w_out_ab, lam_q1, lam_k1, lam_q2, lam_k2,
           g_sub_a, rpb_b, w_down_c, g_q_c, g_kv_c, w_uq_c, w_uk_c, w_uv_c, w_out_c,
           w_group_router, b_group_router, w_expert_router, b_expert_router, w1_moe, w3_moe, w2_moe):
    bc, seq, d = x_prompt.shape
    bs, s_lat, _ = x_sample.shape
    past = cache_a_k.shape[2]
    depth = w_ada.shape[0]
    tc, ts = bc * seq, bs * s_lat
    t = tc + ts
    h_a, h_b = cache_a_k.shape[3], cache_b_k.shape[3]
    ab = (h_a + h_b) * HEAD_DIM
    dims = (bc, seq, bs, s_lat, past)
    assert tc % s_lat == 0 and t % past == 0 and s_lat % (2 * ROW_TILE) == 0

    x = jnp.concatenate([x_prompt.reshape(tc, d), x_sample.reshape(ts, d)], axis=0)
    cond = jnp.concatenate([c_ctx[None], c, jnp.zeros((COND_ROWS_PAD - 1 - bs, d), f32)], axis=0)
    mod = _ada_mod(cond, w_ada, b_ada)
    tables = _rope_tables(s_lat)

    n_exp = N_GROUPS * EXPERTS_PER_GROUP
    f = w1_moe.shape[-1]
    w1_all = w1_moe.reshape(depth, n_exp, d, f).astype(bf16)
    w3_all = w3_moe.reshape(depth, n_exp, d, f).astype(bf16)
    w2_all = w2_moe.reshape(depth, n_exp, f, d).astype(bf16)
    sak, sav, sbk, sbv, sckv, sckr = [], [], [], [], [], []
    moe_pending = None
    for l in range(depth):
        mod4 = mod[l].reshape(COND_ROWS_PAD, 6, 1, d)
        if moe_pending is None:
            h = _norm(x, g_norm1[l], mod4, tc, s_lat, sel=0)[0]
        else:
            x, h = _norm(x, g_norm1[l], mod4, tc, s_lat, sel=0, moe=moe_pending)
        if l % 2 == 0:
            e = l // 2
            lam_init = 0.8 - 0.6 * math.exp(-0.3 * l)
            lam = (jnp.exp(jnp.sum((lam_q1[e] * lam_k1[e]).astype(f32)))
                   - jnp.exp(jnp.sum((lam_q2[e] * lam_k2[e]).astype(f32))) + lam_init)
            p = _matmul(h, w_in_ab[e].astype(bf16), f32)
            wa = h_a * HEAD_DIM
            sak.append(p[:tc, wa:2 * wa].reshape(bc, seq, h_a, HEAD_DIM))
            sav.append(p[:tc, 2 * wa:3 * wa].reshape(bc, seq, h_a, HEAD_DIM))
            wb = h_b * HEAD_DIM
            sbk.append(p[:tc, 3 * wa + wb:3 * wa + 2 * wb].reshape(bc, seq, h_b, HEAD_DIM))
            sbv.append(p[:tc, 3 * wa + 2 * wb:].reshape(bc, seq, h_b, HEAD_DIM))
            o = _attn_even(p, ab, lam, lam_init, g_sub_a[e], rpb_b[e],
                           cache_a_k[:, e].reshape(bs, past, wa), cache_a_v[:, e].reshape(bs, past, wa),
                           cache_b_k[:, e].reshape(bs, past, wb), cache_b_v[:, e].reshape(bs, past, wb),
                           dims, tables)
            w_out = w_out_ab[e].astype(bf16)
        else:
            oi = l // 2
            q_lora, kv_lora = g_q_c.shape[1], g_kv_c.shape[1]
            h_c = w_uk_c.shape[2] // NOPE_DIM
            wd = w_down_c[oi]
            wd = jnp.concatenate([wd, wd[:, q_lora + kv_lora:]], axis=1).astype(bf16)
            cq, ckv, krd = _mla_down(h, wd, g_q_c[oi], g_kv_c[oi])
            sckv.append(ckv[:tc].reshape(bc, seq, kv_lora))
            sckr.append(krd[:tc, :ROPE_DIM].reshape(bc, seq, ROPE_DIM))
            wq = w_uq_c[oi].reshape(q_lora, h_c, NOPE_DIM + ROPE_DIM)
            wq = jnp.concatenate([wq[:, :, :NOPE_DIM].reshape(q_lora, -1), wq[:, :, NOPE_DIM:].reshape(q_lora, -1)],
                                 axis=1).astype(bf16)
            q = _matmul(cq, wq, f32)
            ckv_all = jnp.concatenate([ckv, cache_c_kv[:, oi].reshape(bs * past, kv_lora)], axis=0).astype(bf16)
            wkv = jnp.concatenate([w_uk_c[oi], w_uv_c[oi]], axis=1).astype(bf16)
            kv = _matmul(ckv_all, wkv, bf16)
            krc = cache_c_kr[:, oi]
            o = _attn_mla(q, kv, krd, jnp.concatenate([krc, krc], axis=-1), h_c, dims, tables)
            w_out = w_out_c[oi].astype(bf16)
        x = _matmul(o, w_out, f32, residual=(x, mod4, 2, tc, s_lat))
        w_r = jnp.concatenate([w_group_router[l], w_expert_router[l]], axis=1)
        w_r = jnp.pad(w_r, ((0, 0), (0, LANES - w_r.shape[1]))).astype(bf16)
        b_r = jnp.pad(jnp.concatenate([b_group_router[l], b_expert_router[l]]), (0, LANES - N_GROUPS - n_exp))
        hp, eid, wgt, cnt = _norm(x, g_norm2[l], mod4, tc, s_lat, sel=1,
                                  route=(w_r, b_r.reshape(1, LANES).astype(f32)))
        counts = jnp.sum(cnt, axis=(0, 1))[N_GROUPS:N_GROUPS + n_exp]
        y = _moe(hp, eid[:, :2], wgt[:, :2], counts, w1_all, w3_all, w2_all, l)
        moe_pending = (y, mod4, 5)
    y_prompt = _norm(x, g_final, None, tc, s_lat, moe=moe_pending, final=True, rows=(0, tc))[0].reshape(bc, seq, d)
    y_sample = _norm(x, g_final, None, tc, s_lat, moe=moe_pending, final=True, rows=(tc, ts))[0].reshape(bs, s_lat, d)
    return (y_prompt, y_sample, jnp.stack(sak, axis=1), jnp.stack(sav, axis=1), jnp.stack(sbk, axis=1),
            jnp.stack(sbv, axis=1), jnp.stack(sckv, axis=1), jnp.stack(sckr, axis=1))
```

```python
import functools
import math

import numpy as np
import jax
import jax.numpy as jnp
from jax import lax
from jax.experimental import pallas as pl
from jax.experimental.pallas import tpu as pltpu

GRID_W = 64
HEAD_DIM = 128
WIN_ROWS = 8
WIN_COLS = 16
NOPE_DIM = 128
ROPE_DIM = 64
V_DIM = 128
N_GROUPS = 4
EXPERTS_PER_GROUP = 8
ROPE_THETA = 10000.0
NORM_EPS = 1e-6
NEG_INF = -1e30
LOG2E = math.log2(math.e)

LANES = 128
COND_ROWS_PAD = 16
VMEM_LIMIT = 56 * 1024 * 1024
ROW_TILE = 512
COL_TILE = 512
NORM_TILE = 256
MOE_TILE = 256
Q_TILE = 512
NBR_ROWS = 4
CTX_HEADS = 8
NBR_HEADS = 2

f32 = jnp.float32
bf16 = jnp.bfloat16


def _cp(*sem):
    return pltpu.CompilerParams(dimension_semantics=sem, vmem_limit_bytes=VMEM_LIMIT)


def _pick(n, pref):
    if n <= pref:
        return n
    t = pref
    while n % t:
        t //= 2
    return t


def _cond_row(r0, tc, s_lat):
    return jnp.where(r0 < tc, 0, 1 + (r0 - tc) // s_lat)


def _dot_nt(a, b):
    return lax.dot_general(a, b, (((1,), (1,)), ((), ())), preferred_element_type=f32)


def _ada_kernel(c_ref, w_ref, b_ref, o_ref):
    s = jax.nn.silu(c_ref[...]).astype(bf16)
    o_ref[...] = jnp.dot(s, w_ref[...].astype(bf16), preferred_element_type=f32) + b_ref[...]


def _ada_mod(cond, w_ada, b_ada):
    depth, d, n = w_ada.shape
    r = cond.shape[0]
    tn = _pick(n, COL_TILE)
    return pl.pallas_call(
        _ada_kernel,
        grid=(depth, n // tn),
        in_specs=[pl.BlockSpec((r, d), lambda l, j: (0, 0)),
                  pl.BlockSpec((None, d, tn), lambda l, j: (l, 0, j)),
                  pl.BlockSpec((None, 1, tn), lambda l, j: (l, 0, j))],
        out_specs=pl.BlockSpec((None, r, tn), lambda l, j: (l, 0, j)),
        out_shape=jax.ShapeDtypeStruct((depth, r, n), f32),
        compiler_params=_cp("parallel", "parallel"),
        name="ada_mod",
    )(cond, w_ada, b_ada.reshape(depth, 1, n))


def _norm_kernel(*refs, combine, final, route):
    it = iter(refs)
    x_ref = next(it)
    if combine:
        y0_ref, y1_ref, gate_ref = next(it), next(it), next(it)
    g_ref = next(it)
    if not final:
        shift_ref, scale_ref = next(it), next(it)
    if route:
        wr_ref, br_ref = next(it), next(it)
    x = x_ref[...]
    if combine:
        x = x + gate_ref[...] * (y0_ref[...] + y1_ref[...])
        if not final:
            next(it)[...] = x
    y = x * lax.rsqrt(jnp.mean(x * x, axis=-1, keepdims=True) + NORM_EPS) * g_ref[...]
    if final:
        next(it)[...] = y
        return
    h = (y * (1.0 + scale_ref[...]) + shift_ref[...]).astype(bf16)
    if not route:
        next(it)[...] = h
        return
    bits = lax.bitcast_convert_type(h.astype(f32), jnp.uint32)
    half = bits.shape[1] // 2
    next(it)[...] = (bits[:, half:] & jnp.uint32(0xFFFF0000)) | (bits[:, :half] >> 16)
    eid, wgt, cnt = _route(jnp.dot(h, wr_ref[...], preferred_element_type=f32) + br_ref[...])
    next(it)[...] = eid
    next(it)[...] = wgt
    next(it)[...] = cnt


def _norm(x, g, mod4, tc, s_lat, *, sel=None, moe=None, final=False, route=None, rows=None):
    t, d = x.shape
    tm = NORM_TILE
    nt = t // tm
    combine = moe is not None
    i0, n_steps = (0, nt) if rows is None else (rows[0] // tm, rows[1] // tm)
    assert rows is None or final
    row = lambda i: (i + i0, 0)
    modspec = lambda j: pl.BlockSpec((None, None, 1, d),
                                     lambda i: (_cond_row((i + i0) * tm, tc, s_lat), j, 0, 0))
    args, specs = [x], [pl.BlockSpec((tm, d), row)]
    if combine:
        y, mod4_moe, gate_j = moe
        args += [y, y, mod4_moe]
        specs += [pl.BlockSpec((tm, d), row), pl.BlockSpec((tm, d), lambda i: (i + i0 + nt, 0)), modspec(gate_j)]
    args.append(g.reshape(1, d))
    specs.append(pl.BlockSpec((1, d), lambda i: (0, 0)))
    if not final:
        args += [mod4, mod4]
        specs += [modspec(3 * sel), modspec(3 * sel + 1)]
    if route is not None:
        args += list(route)
        specs += [pl.BlockSpec((d, LANES), lambda i: (0, 0)), pl.BlockSpec((1, LANES), lambda i: (0, 0))]
    shapes, ospecs = [], []
    if combine and not final:
        shapes.append(jax.ShapeDtypeStruct((t, d), f32))
        ospecs.append(pl.BlockSpec((tm, d), row))
    if final:
        shapes.append(jax.ShapeDtypeStruct((n_steps * tm, d), f32))
        ospecs.append(pl.BlockSpec((tm, d), lambda i: (i, 0)))
    elif route is None:
        shapes.append(jax.ShapeDtypeStruct((t, d), bf16))
        ospecs.append(pl.BlockSpec((tm, d), row))
    else:
        shapes += [jax.ShapeDtypeStruct((t, d // 2), jnp.uint32), jax.ShapeDtypeStruct((t, LANES), jnp.int32),
                   jax.ShapeDtypeStruct((t, LANES), f32), jax.ShapeDtypeStruct((nt, 1, LANES), jnp.int32)]
        ospecs += [pl.BlockSpec((tm, d // 2), row), pl.BlockSpec((tm, LANES), row), pl.BlockSpec((tm, LANES), row),
                   pl.BlockSpec((None, 1, LANES), lambda i: (i, 0, 0))]
    return pl.pallas_call(
        functools.partial(_norm_kernel, combine=combine, final=final, route=route is not None),
        grid=(n_steps,),
        in_specs=specs,
        out_specs=ospecs,
        out_shape=shapes,
        input_output_aliases={0: 0} if (combine and not final) else {},
        compiler_params=_cp("parallel"),
        name="norm",
    )(*args)


def _mm_kernel(*refs, residual):
    if residual:
        a_ref, b_ref, x_ref, gate_ref, o_ref = refs
    else:
        a_ref, b_ref, o_ref = refs
    acc = jnp.dot(a_ref[...], b_ref[...], preferred_element_type=f32)
    if residual:
        acc = x_ref[...] + gate_ref[...] * acc
    o_ref[...] = acc.astype(o_ref.dtype)


def _matmul(a, b, out_dtype, *, residual=None):
    m, k = a.shape
    n = b.shape[1]
    tm = _pick(m, 2 * ROW_TILE if (k >= 4 * ROW_TILE or k <= ROW_TILE) else ROW_TILE)
    tn = _pick(n, COL_TILE * max(2, min(4, 4 * ROW_TILE // k)))
    args = [a, b]
    specs = [pl.BlockSpec((tm, k), lambda i, j: (i, 0)), pl.BlockSpec((k, tn), lambda i, j: (0, j))]
    aliases = {}
    if residual is not None:
        x, mod4, gate_j, tc, s_lat = residual
        args += [x, mod4]
        specs += [pl.BlockSpec((tm, tn), lambda i, j: (i, j)),
                  pl.BlockSpec((None, None, 1, tn), lambda i, j: (_cond_row(i * tm, tc, s_lat), gate_j, 0, j))]
        aliases = {2: 0}
    return pl.pallas_call(
        functools.partial(_mm_kernel, residual=residual is not None),
        grid=(m // tm, n // tn),
        in_specs=specs,
        out_specs=pl.BlockSpec((tm, tn), lambda i, j: (i, j)),
        out_shape=jax.ShapeDtypeStruct((m, n), out_dtype),
        input_output_aliases=aliases,
        compiler_params=_cp("parallel", "parallel"),
        name="matmul",
    )(*args)


def _down_kernel(a_ref, w_ref, gq_ref, gkv_ref, cq_ref, ckv_ref, kr_ref, *, q_lora, kv_lora):
    acc = jnp.dot(a_ref[...], w_ref[...], preferred_element_type=f32)
    cq = acc[:, :q_lora]
    cq = cq * lax.rsqrt(jnp.mean(cq * cq, axis=-1, keepdims=True) + NORM_EPS) * gq_ref[...]
    cq_ref[...] = cq.astype(bf16)
    ckv = acc[:, q_lora:q_lora + kv_lora]
    ckv_ref[...] = ckv * lax.rsqrt(jnp.mean(ckv * ckv, axis=-1, keepdims=True) + NORM_EPS) * gkv_ref[...]
    kr_ref[...] = acc[:, q_lora + kv_lora:]


def _mla_down(h, w_down_dup, g_q, g_kv):
    t, d = h.shape
    q_lora, kv_lora = g_q.shape[0], g_kv.shape[0]
    n = w_down_dup.shape[1]
    tm = _pick(t, ROW_TILE)
    return pl.pallas_call(
        functools.partial(_down_kernel, q_lora=q_lora, kv_lora=kv_lora),
        grid=(t // tm,),
        in_specs=[pl.BlockSpec((tm, d), lambda i: (i, 0)),
                  pl.BlockSpec((d, n), lambda i: (0, 0)),
                  pl.BlockSpec((1, q_lora), lambda i: (0, 0)),
                  pl.BlockSpec((1, kv_lora), lambda i: (0, 0))],
        out_specs=[pl.BlockSpec((tm, q_lora), lambda i: (i, 0)),
                   pl.BlockSpec((tm, kv_lora), lambda i: (i, 0)),
                   pl.BlockSpec((tm, LANES), lambda i: (i, 0))],
        out_shape=[jax.ShapeDtypeStruct((t, q_lora), bf16),
                   jax.ShapeDtypeStruct((t, kv_lora), f32),
                   jax.ShapeDtypeStruct((t, LANES), f32)],
        compiler_params=_cp("parallel"),
        name="mla_down",
    )(h, w_down_dup, g_q.reshape(1, -1), g_kv.reshape(1, -1))


def _rope_tables(s_lat):
    half = ROPE_DIM // 2
    quarter = half // 2
    tpos = jnp.arange(s_lat)
    inv = ROPE_THETA ** (-jnp.arange(0, half, 2, dtype=f32) / half)
    ang_r = (tpos // GRID_W).astype(f32)[:, None] * inv
    ang_c = (tpos % GRID_W).astype(f32)[:, None] * inv
    ang = jnp.concatenate([ang_r, ang_r, ang_c, ang_c], axis=-1)
    ang = jnp.tile(ang, (1, LANES // ROPE_DIM))
    first = (jnp.arange(LANES) % half) < quarter
    sin = jnp.sin(ang)
    return jnp.cos(ang), jnp.where(first, -sin, 0.0), jnp.where(first, 0.0, sin)


def _rope(x, cos, sin_up, sin_dn):
    quarter = ROPE_DIM // 4
    return x * cos + pltpu.roll(x, LANES - quarter, 1) * sin_up + pltpu.roll(x, quarter, 1) * sin_dn


def _softmax_parts(s):
    m = jnp.max(s, axis=-1, keepdims=True)
    e = jnp.exp2(s - m)
    return e, jnp.sum(e, axis=-1, keepdims=True)


def _diff_logits(q, kk):
    dqk = HEAD_DIM // 2
    q = q * (dqk ** -0.5 * LOG2E)
    lane = lax.broadcasted_iota(jnp.int32, q.shape, 1)
    return (_dot_nt(jnp.where(lane < dqk, q, 0.0).astype(bf16), kk),
            _dot_nt(jnp.where(lane >= dqk, q, 0.0).astype(bf16), kk))


def _diff_finish(s1, m1, s2, m2, vv, lam, g, lam_init):
    e1, e2 = jnp.exp2(s1 - m1), jnp.exp2(s2 - m2)
    l1, l2 = jnp.sum(e1, axis=-1, keepdims=True), jnp.sum(e2, axis=-1, keepdims=True)
    a = e1 * (1.0 / l1) - e2 * (lam / l2)
    o = jnp.dot(a.astype(bf16), vv, preferred_element_type=f32)
    o = o * lax.rsqrt(jnp.mean(o * o, axis=-1, keepdims=True) + NORM_EPS) * g
    return (o * (1.0 - lam_init)).astype(bf16)


def _diff_finish_aug(s1, m1, s2, m2, vaug, lam, g, lam_init):
    o1 = jnp.dot(jnp.exp2(s1 - m1).astype(bf16), vaug, preferred_element_type=f32)
    o2 = jnp.dot(jnp.exp2(s2 - m2).astype(bf16), vaug, preferred_element_type=f32)
    hd = HEAD_DIM
    o = o1[:, :hd] * (1.0 / o1[:, hd:hd + 1]) - o2[:, :hd] * (lam / o2[:, hd:hd + 1])
    o = o * lax.rsqrt(jnp.mean(o * o, axis=-1, keepdims=True) + NORM_EPS) * g
    return (o * (1.0 - lam_init)).astype(bf16)


def _diff_ctx_kernel(lam_ref, q_ref, k_ref, v_ref, g_ref, o_ref, *, nh, lam_init):
    for hh in range(nh):
        cols = slice(hh * HEAD_DIM, (hh + 1) * HEAD_DIM)
        s1, s2 = _diff_logits(q_ref[:, cols], k_ref[:, cols].astype(bf16))
        o_ref[:, cols] = _diff_finish(s1, jnp.max(s1, axis=-1, keepdims=True), s2, jnp.max(s2, axis=-1, keepdims=True),
                                      v_ref[:, cols].astype(bf16), lam_ref[0], g_ref[...], lam_init)


def _diff_lat_kernel(lam_ref, q_ref, q_nx_ref, k_ref, v_ref, kc_ref, vc_ref, cos_ref, sup_ref, sdn_ref, g_ref, o_ref,
                     kall, vall, s0, s1, m0, m1, *, tq, s_self, s_cache, nq, lam_init):
    qi = pl.program_id(2)
    bufs = ((s0, m0), (s1, m1))

    def logits(q_r, tile, s_buf, m_buf):
        rows = pl.ds(pl.multiple_of(tile * tq, tq), tq)
        q = _rope(q_r[...], cos_ref[rows, :], sup_ref[rows, :], sdn_ref[rows, :])
        for c, s in enumerate(_diff_logits(q, kall[...])):
            s_buf[c] = s
            m_buf[c] = jnp.max(s, axis=-1, keepdims=True)

    @pl.when(qi == 0)
    def _():
        kall[0:s_self, :] = _rope(k_ref[...], cos_ref[...], sup_ref[...], sdn_ref[...]).astype(bf16)
        kall[s_self:s_self + s_cache, :] = kc_ref[...].astype(bf16)
        vall[0:s_self, 0:HEAD_DIM] = v_ref[...].astype(bf16)
        vall[s_self:s_self + s_cache, 0:HEAD_DIM] = vc_ref[...].astype(bf16)
        vall[:, HEAD_DIM:2 * HEAD_DIM] = jnp.ones((s_self + s_cache, HEAD_DIM), bf16)
        logits(q_ref, 0, s0, m0)

    for parity in (0, 1):
        (s_cur, m_cur), (s_nxt, m_nxt) = bufs[parity], bufs[1 - parity]

        @pl.when(qi % 2 == parity)
        def _():
            logits(q_nx_ref, jnp.minimum(qi + 1, nq - 1), s_nxt, m_nxt)
            o_ref[...] = _diff_finish_aug(s_cur[0], m_cur[0], s_cur[1], m_cur[1], vall[...], lam_ref[0],
                                          g_ref[...], lam_init)


def _soft_ctx_kernel(q_ref, k_ref, v_ref, o_ref, *, nh):
    for hh in range(nh):
        cols = slice(hh * HEAD_DIM, (hh + 1) * HEAD_DIM)
        q = (q_ref[:, cols] * (HEAD_DIM ** -0.5 * LOG2E)).astype(bf16)
        e, l = _softmax_parts(_dot_nt(q, k_ref[:, cols].astype(bf16)))
        o = jnp.dot(e.astype(bf16), v_ref[:, cols].astype(bf16), preferred_element_type=f32)
        o_ref[:, cols] = (o * (1.0 / l)).astype(bf16)


def _nbr_kernel(ks_ref, cfg_ref, q_ref, k_ref, v_ref, kc_ref, vc_ref, bias_ref, o_ref, *, kw, nh):
    del cfg_ref
    t = pl.program_id(2)
    rows = pl.ds(pl.multiple_of(ks_ref[t] * GRID_W, GRID_W), kw)
    for hh in range(nh):
        c = slice(hh * HEAD_DIM, (hh + 1) * HEAD_DIM)
        q = (q_ref[:, c] * (HEAD_DIM ** -0.5 * LOG2E)).astype(bf16)
        s_loc = _dot_nt(q, k_ref[rows, c].astype(bf16)) + bias_ref[hh]
        s_ctx = _dot_nt(q, kc_ref[:, c].astype(bf16))
        m = jnp.maximum(jnp.max(s_loc, axis=-1, keepdims=True), jnp.max(s_ctx, axis=-1, keepdims=True))
        e_loc, e_ctx = jnp.exp2(s_loc - m), jnp.exp2(s_ctx - m)
        l = jnp.sum(e_loc, axis=-1, keepdims=True) + jnp.sum(e_ctx, axis=-1, keepdims=True)
        o = (jnp.dot(e_loc.astype(bf16), v_ref[rows, c].astype(bf16), preferred_element_type=f32)
             + jnp.dot(e_ctx.astype(bf16), vc_ref[:, c].astype(bf16), preferred_element_type=f32))
        o_ref[:, c] = (o * (1.0 / l)).astype(bf16)


def _mla_head(qn, qr_masked, kcat, vv):
    qc = (jnp.concatenate([qn, qr_masked], axis=-1) * ((NOPE_DIM + ROPE_DIM) ** -0.5 * LOG2E)).astype(bf16)
    e, l = _softmax_parts(_dot_nt(qc, kcat))
    o = jnp.dot(e.astype(bf16), vv, preferred_element_type=f32)
    return (o * (1.0 / l)).astype(bf16)


def _mla_ctx_kernel(qn_ref, qr_ref, kn_ref, v_ref, kr_ref, o_ref, *, npair):
    hd = NOPE_DIM
    kr = kr_ref[...].astype(bf16)
    lane = lax.broadcasted_iota(jnp.int32, (qr_ref.shape[0], LANES), 1)
    for pp in range(npair):
        qr = qr_ref[:, pp * LANES:(pp + 1) * LANES]
        for hh in range(2):
            c = slice((2 * pp + hh) * hd, (2 * pp + hh + 1) * hd)
            mask = (lane < ROPE_DIM) if hh == 0 else (lane >= ROPE_DIM)
            kcat = jnp.concatenate([kn_ref[:, c], kr], axis=-1)
            o_ref[:, c] = _mla_head(qn_ref[:, c], jnp.where(mask, qr, 0.0), kcat, v_ref[:, c])


def _mla_lat_kernel(qn_ref, qr_ref, qn_nx_ref, qr_nx_ref, kn_ref, v_ref, kr_ref, knc_ref, vc_ref, krc_ref,
                    cos_ref, sup_ref, sdn_ref, o_ref, kcat, vaug, s0, s1, m0, m1, *, tq, s_self, s_cache, nq):
    qi = pl.program_id(2)
    hd = NOPE_DIM
    s_all = s_self + s_cache
    bufs = ((s0, m0), (s1, m1))

    def logits(qn_r, qr_r, tile, s_buf, m_buf):
        rows = pl.ds(pl.multiple_of(tile * tq, tq), tq)
        qr = _rope(qr_r[...], cos_ref[rows, :], sup_ref[rows, :], sdn_ref[rows, :])
        lane = lax.broadcasted_iota(jnp.int32, qr.shape, 1)
        for hh in range(2):
            mask = (lane < ROPE_DIM) if hh == 0 else (lane >= ROPE_DIM)
            qc = jnp.concatenate([qn_r[:, hh * hd:(hh + 1) * hd], jnp.where(mask, qr, 0.0)], axis=-1)
            s = _dot_nt((qc * ((NOPE_DIM + ROPE_DIM) ** -0.5 * LOG2E)).astype(bf16), kcat[hh])
            s_buf[hh] = s
            m_buf[hh] = jnp.max(s, axis=-1, keepdims=True)

    @pl.when(qi == 0)
    def _():
        kr = _rope(kr_ref[...], cos_ref[...], sup_ref[...], sdn_ref[...]).astype(bf16)
        krc = krc_ref[...].astype(bf16)
        ones = jnp.ones((s_all, V_DIM), bf16)
        for hh in range(2):
            c = slice(hh * hd, (hh + 1) * hd)
            kcat[hh, 0:s_self, 0:hd] = kn_ref[:, c]
            kcat[hh, 0:s_self, hd:2 * hd] = kr
            kcat[hh, s_self:s_all, 0:hd] = knc_ref[:, c]
            kcat[hh, s_self:s_all, hd:2 * hd] = krc
            vaug[hh, 0:s_self, 0:V_DIM] = v_ref[:, c]
            vaug[hh, s_self:s_all, 0:V_DIM] = vc_ref[:, c]
            vaug[hh, :, V_DIM:2 * V_DIM] = ones
        logits(qn_ref, qr_ref, 0, s0, m0)

    for parity in (0, 1):
        (s_cur, m_cur), (s_nxt, m_nxt) = bufs[parity], bufs[1 - parity]

        @pl.when(qi % 2 == parity)
        def _():
            logits(qn_nx_ref, qr_nx_ref, jnp.minimum(qi + 1, nq - 1), s_nxt, m_nxt)
            for hh in range(2):
                e = jnp.exp2(s_cur[hh] - m_cur[hh]).astype(bf16)
                oa = jnp.dot(e, vaug[hh], preferred_element_type=f32)
                o_ref[:, hh * V_DIM:(hh + 1) * V_DIM] = (oa[:, :V_DIM] * (1.0 / oa[:, V_DIM:V_DIM + 1])).astype(bf16)


def _nbr_tables(rows):
    r_tile = min(NBR_ROWS, rows)
    kr = min(WIN_ROWS, rows)
    krw = min(rows, r_tile + kr)
    n_rel = 2 * WIN_ROWS - 1
    ks_list, cfg_list, cfgs = [], [], {}
    for t in range(rows // r_tile):
        qr = t * r_tile + np.arange(r_tile)
        r0 = np.clip(qr - kr // 2, 0, rows - kr)
        ks = int(np.clip(t * r_tile - kr // 2, 0, rows - krw))
        krow = ks + np.arange(krw)
        row_ok = (krow[None, :] >= r0[:, None]) & (krow[None, :] < r0[:, None] + kr)
        rel = np.where(row_ok, krow[None, :] - qr[:, None] + WIN_ROWS - 1, n_rel).astype(np.int32)
        key = rel.tobytes()
        if key not in cfgs:
            cfgs[key] = (len(cfgs), rel)
        ks_list.append(ks)
        cfg_list.append(cfgs[key][0])
    rel_all = np.stack([c[1] for c in sorted(cfgs.values(), key=lambda c: c[0])])
    return r_tile, krw, np.asarray(ks_list, np.int32), np.asarray(cfg_list, np.int32), rel_all


def _nbr_bias(rpb, rel_all):
    w = GRID_W
    kc = min(WIN_COLS, w)
    h, n_rel, _ = rpb.shape
    span = 2 * w - 1
    off = np.clip(np.arange(span) - (w - 1), -(WIN_COLS - 1), WIN_COLS - 1) + WIN_COLS - 1
    v = rpb[:, :, off].astype(f32) * LOG2E
    m = jnp.tile(v, (1, 1, w + 1))[:, :, :w * (span + 1)].reshape(h, n_rel, w, span + 1)[:, :, :, :w]
    toep = m[:, :, ::-1, :]
    cols = np.arange(w)
    col_start = np.clip(cols - kc // 2, 0, w - kc)
    col_ok = (cols[None, :] >= col_start[:, None]) & (cols[None, :] < col_start[:, None] + kc)
    toep = jnp.where(col_ok, toep, NEG_INF)
    toep = jnp.concatenate([toep, jnp.full((h, 1, w, w), NEG_INF, f32)], axis=1)
    n_cfg, r_tile, krw = rel_all.shape
    b = toep[:, rel_all.reshape(-1)].reshape(h, n_cfg, r_tile, krw, w, w)
    return b.transpose(0, 1, 2, 4, 3, 5).reshape(h, n_cfg, r_tile * w, krw * w)


def _attn_even(p, o_width, lam, lam_init, g_sub, rpb, ck_a, cv_a, ck_b, cv_b, dims, tables):
    bc, seq, bs, s_lat, past = dims
    t = p.shape[0]
    tc = bc * seq
    h_a = ck_a.shape[-1] // HEAD_DIM
    h_b = ck_b.shape[-1] // HEAD_DIM
    cos, sup, sdn = tables
    hd = HEAD_DIM
    lam_arr = jnp.reshape(lam, (1,)).astype(f32)
    g2 = g_sub.reshape(1, hd)
    smem = pl.BlockSpec(memory_space=pltpu.SMEM)
    o_shape = jax.ShapeDtypeStruct((t, o_width), bf16)

    nh = _pick(h_a, CTX_HEADS)
    ga = h_a // nh
    wb = nh * hd
    o = pl.pallas_call(
        functools.partial(_diff_ctx_kernel, nh=nh, lam_init=lam_init),
        grid=(bc, ga),
        in_specs=[smem,
                  pl.BlockSpec((seq, wb), lambda b, h: (b, h)),
                  pl.BlockSpec((seq, wb), lambda b, h: (b, ga + h)),
                  pl.BlockSpec((seq, wb), lambda b, h: (b, 2 * ga + h)),
                  pl.BlockSpec((1, hd), lambda b, h: (0, 0))],
        out_specs=pl.BlockSpec((seq, wb), lambda b, h: (b, h)),
        out_shape=o_shape,
        compiler_params=_cp("parallel", "parallel"),
        name="attn_diff_ctx",
    )(lam_arr, p, p, p, g2)

    nhb = _pick(h_b, CTX_HEADS)
    gb = h_b // nhb
    wbb = nhb * hd
    base = 3 * h_a * hd // wbb
    o = pl.pallas_call(
        lambda o_in, q, k, v, o_out: _soft_ctx_kernel(q, k, v, o_out, nh=nhb),
        grid=(bc, gb),
        in_specs=[pl.BlockSpec(memory_space=pl.ANY),
                  pl.BlockSpec((seq, wbb), lambda b, h: (b, base + h)),
                  pl.BlockSpec((seq, wbb), lambda b, h: (b, base + gb + h)),
                  pl.BlockSpec((seq, wbb), lambda b, h: (b, base + 2 * gb + h))],
        out_specs=pl.BlockSpec((seq, wbb), lambda b, h: (b, h_a * hd // wbb + h)),
        out_shape=o_shape,
        input_output_aliases={0: 0},
        compiler_params=_cp("parallel", "parallel"),
        name="attn_soft_ctx",
    )(o, p, p, p)

    tq = _pick(s_lat, Q_TILE)
    nq = s_lat // tq
    full = lambda b, h, i: (0, 0)
    o = pl.pallas_call(
        lambda o_in, *refs: _diff_lat_kernel(*refs, tq=tq, s_self=s_lat, s_cache=past, nq=nq, lam_init=lam_init),
        grid=(bs, h_a, nq),
        in_specs=[pl.BlockSpec(memory_space=pl.ANY), smem,
                  pl.BlockSpec((tq, hd), lambda b, h, i: (tc // tq + b * nq + i, h)),
                  pl.BlockSpec((tq, hd), lambda b, h, i: (tc // tq + b * nq + jnp.minimum(i + 1, nq - 1), h)),
                  pl.BlockSpec((s_lat, hd), lambda b, h, i: (tc // s_lat + b, h_a + h)),
                  pl.BlockSpec((s_lat, hd), lambda b, h, i: (tc // s_lat + b, 2 * h_a + h)),
                  pl.BlockSpec((None, past, hd), lambda b, h, i: (b, 0, h)),
                  pl.BlockSpec((None, past, hd), lambda b, h, i: (b, 0, h)),
                  pl.BlockSpec((s_lat, hd), full), pl.BlockSpec((s_lat, hd), full), pl.BlockSpec((s_lat, hd), full),
                  pl.BlockSpec((1, hd), full)],
        out_specs=pl.BlockSpec((tq, hd), lambda b, h, i: (tc // tq + b * nq + i, h)),
        out_shape=o_shape,
        scratch_shapes=[pltpu.VMEM((s_lat + past, hd), bf16), pltpu.VMEM((s_lat + past, 2 * hd), bf16),
                        pltpu.VMEM((2, tq, s_lat + past), f32), pltpu.VMEM((2, tq, s_lat + past), f32),
                        pltpu.VMEM((2, tq, 1), f32), pltpu.VMEM((2, tq, 1), f32)],
        input_output_aliases={0: 0},
        compiler_params=_cp("parallel", "parallel", "arbitrary"),
        name="attn_diff_lat",
    )(o, lam_arr, p, p, p, p, ck_a, cv_a, cos, sup, sdn, g2)

    rows = s_lat // GRID_W
    r_tile, krw, ks_np, cfg_np, rel_all = _nbr_tables(rows)
    tqn, kw = r_tile * GRID_W, krw * GRID_W
    nqn = rows // r_tile
    bias = _nbr_bias(rpb, rel_all)
    nhn = _pick(math.gcd(h_a, h_b), NBR_HEADS)
    wn = nhn * hd
    gn, ga_n = h_b // nhn, h_a // nhn
    o = pl.pallas_call(
        lambda ks, cfg, o_in, *refs: _nbr_kernel(ks, cfg, *refs, kw=kw, nh=nhn),
        grid_spec=pltpu.PrefetchScalarGridSpec(
            num_scalar_prefetch=2,
            grid=(bs, gn, nqn),
            in_specs=[pl.BlockSpec(memory_space=pl.ANY),
                      pl.BlockSpec((tqn, wn), lambda b, h, i, ks, cfg: (tc // tqn + b * nqn + i, 3 * ga_n + h)),
                      pl.BlockSpec((s_lat, wn), lambda b, h, i, ks, cfg: (tc // s_lat + b, 3 * ga_n + gn + h)),
                      pl.BlockSpec((s_lat, wn), lambda b, h, i, ks, cfg: (tc // s_lat + b, 3 * ga_n + 2 * gn + h)),
                      pl.BlockSpec((None, past, wn), lambda b, h, i, ks, cfg: (b, 0, h)),
                      pl.BlockSpec((None, past, wn), lambda b, h, i, ks, cfg: (b, 0, h)),
                      pl.BlockSpec((nhn, None, tqn, kw), lambda b, h, i, ks, cfg: (h, cfg[i], 0, 0))],
            out_specs=pl.BlockSpec((tqn, wn), lambda b, h, i, ks, cfg: (tc // tqn + b * nqn + i, ga_n + h)),
        ),
        out_shape=o_shape,
        input_output_aliases={2: 0},
        compiler_params=_cp("parallel", "parallel", "arbitrary"),
        name="attn_nbr_lat",
    )(jnp.asarray(ks_np), jnp.asarray(cfg_np), o, p, p, p, ck_b, cv_b, bias)
    return o


def _attn_mla(q, kv, krd, krd_cache, h_c, dims, tables):
    bc, seq, bs, s_lat, past = dims
    t = q.shape[0]
    tc = bc * seq
    cos, sup, sdn = tables
    hp = h_c // 2
    w2 = 2 * NOPE_DIM
    o_shape = jax.ShapeDtypeStruct((t, h_c * V_DIM), bf16)

    npair = _pick(hp, CTX_HEADS // 2)
    gp = hp // npair
    wn, wr = npair * w2, npair * LANES
    o = pl.pallas_call(
        functools.partial(_mla_ctx_kernel, npair=npair),
        grid=(bc, gp),
        in_specs=[pl.BlockSpec((seq, wn), lambda b, h: (b, h)),
                  pl.BlockSpec((seq, wr), lambda b, h: (b, 2 * gp + h)),
                  pl.BlockSpec((seq, wn), lambda b, h: (b, h)),
                  pl.BlockSpec((seq, wn), lambda b, h: (b, gp + h)),
                  pl.BlockSpec((seq, LANES), lambda b, h: (b, 0))],
        out_specs=pl.BlockSpec((seq, wn), lambda b, h: (b, h)),
        out_shape=o_shape,
        compiler_params=_cp("parallel", "parallel"),
        name="attn_mla_ctx",
    )(q, q, kv, kv, krd)

    tq = _pick(s_lat, Q_TILE)
    nq = s_lat // tq
    full = lambda b, h, i: (0, 0)
    o = pl.pallas_call(
        lambda o_in, *refs: _mla_lat_kernel(*refs, tq=tq, s_self=s_lat, s_cache=past, nq=nq),
        grid=(bs, hp, nq),
        in_specs=[pl.BlockSpec(memory_space=pl.ANY),
                  pl.BlockSpec((tq, w2), lambda b, h, i: (tc // tq + b * nq + i, h)),
                  pl.BlockSpec((tq, LANES), lambda b, h, i: (tc // tq + b * nq + i, 2 * hp + h)),
                  pl.BlockSpec((tq, w2), lambda b, h, i: (tc // tq + b * nq + jnp.minimum(i + 1, nq - 1), h)),
                  pl.BlockSpec((tq, LANES),
                               lambda b, h, i: (tc // tq + b * nq + jnp.minimum(i + 1, nq - 1), 2 * hp + h)),
                  pl.BlockSpec((s_lat, w2), lambda b, h, i: (tc // s_lat + b, h)),
                  pl.BlockSpec((s_lat, w2), lambda b, h, i: (tc // s_lat + b, hp + h)),
                  pl.BlockSpec((s_lat, LANES), lambda b, h, i: (tc // s_lat + b, 0)),
                  pl.BlockSpec((past, w2), lambda b, h, i: (t // past + b, h)),
                  pl.BlockSpec((past, w2), lambda b, h, i: (t // past + b, hp + h)),
                  pl.BlockSpec((None, past, LANES), lambda b, h, i: (b, 0, 0)),
                  pl.BlockSpec((s_lat, LANES), full), pl.BlockSpec((s_lat, LANES), full),
                  pl.BlockSpec((s_lat, LANES), full)],
        out_specs=pl.BlockSpec((tq, w2), lambda b, h, i: (tc // tq + b * nq + i, h)),
        out_shape=o_shape,
        scratch_shapes=[pltpu.VMEM((2, s_lat + past, w2), bf16), pltpu.VMEM((2, s_lat + past, 2 * V_DIM), bf16),
                        pltpu.VMEM((2, tq, s_lat + past), f32), pltpu.VMEM((2, tq, s_lat + past), f32),
                        pltpu.VMEM((2, tq, 1), f32), pltpu.VMEM((2, tq, 1), f32)],
        input_output_aliases={0: 0},
        compiler_params=_cp("parallel", "parallel", "arbitrary"),
        name="attn_mla_lat",
    )(o, q, q, q, q, kv, kv, krd, kv, kv, krd_cache, cos, sup, sdn)
    return o


def _route(logits):
    lane = lax.broadcasted_iota(jnp.int32, logits.shape, 1)
    big = jnp.int32(LANES)
    neg = -jnp.inf

    def top(vals):
        v = jnp.max(vals, axis=-1, keepdims=True)
        return v, jnp.min(jnp.where(vals == v, lane, big), axis=-1, keepdims=True)

    gl = jnp.where(lane < N_GROUPS, logits, neg)
    gmax, g_idx = top(gl)
    p_group = 1.0 / jnp.sum(jnp.exp(gl - gmax), axis=-1, keepdims=True)
    lo = N_GROUPS + g_idx * EXPERTS_PER_GROUP
    el = jnp.where((lane >= lo) & (lane < lo + EXPERTS_PER_GROUP), logits, neg)
    v1, i1 = top(el)
    v2, i2 = top(jnp.where(lane == i1, neg, el))
    e = jnp.exp(v2 - v1)
    w1 = p_group / (1.0 + e)
    eid = jnp.where(lane == 0, i1 - N_GROUPS, jnp.where(lane == 1, i2 - N_GROUPS, 0))
    wgt = jnp.where(lane == 0, w1, jnp.where(lane == 1, w1 * e, 0.0))
    cnt = jnp.sum(((lane == i1) | (lane == i2)).astype(jnp.int32), axis=0, keepdims=True)
    return eid, wgt, cnt


def _moe_kernel(te_ref, nu_ref, tok_ref, dst_ref, hp_ref, rw_ref, w1_ref, w3_ref, w2_ref, y_ref,
                xg0, xg1, yb0, yb1, gsem, ssem, *, tm):
    del te_ref
    i = pl.program_id(0)
    n_used = nu_ref[0]
    xg, yb = (xg0, xg1), (yb0, yb1)

    def gather_copy(row, r, slot):
        return pltpu.make_async_copy(hp_ref.at[pl.ds(row, 1)], xg[slot].at[pl.ds(r, 1)], gsem.at[slot])

    def scatter_copy(row, r, slot):
        return pltpu.make_async_copy(yb[slot].at[pl.ds(r, 1)], y_ref.at[pl.ds(row, 1)], ssem.at[slot])

    def gather_wait(slot):
        pltpu.make_async_copy(hp_ref.at[pl.ds(0, tm)], xg[slot], gsem.at[slot]).wait()

    def scatter_wait(slot):
        pltpu.make_async_copy(yb[slot], y_ref.at[pl.ds(0, tm)], ssem.at[slot]).wait()

    def compute(slot):
        u = xg[slot][...]
        half = u.shape[1]
        x_lo = lax.bitcast_convert_type(u << 16, f32).astype(bf16)
        x_hi = lax.bitcast_convert_type(u & jnp.uint32(0xFFFF0000), f32).astype(bf16)
        a = (jnp.dot(x_lo, w1_ref[0:half, :], preferred_element_type=f32)
             + jnp.dot(x_hi, w1_ref[half:2 * half, :], preferred_element_type=f32))
        b = (jnp.dot(x_lo, w3_ref[0:half, :], preferred_element_type=f32)
             + jnp.dot(x_hi, w3_ref[half:2 * half, :], preferred_element_type=f32))
        hid = (jax.nn.silu(a) * b * rw_ref[...]).astype(bf16)
        yb[slot][...] = jnp.dot(hid, w2_ref[...], preferred_element_type=f32)

    @pl.when(i == 0)
    def _():
        yb1[...] = jnp.zeros_like(yb1)

        def prime(r, carry):
            gather_copy(tok_ref[r], r, 0).start()
            return carry

        lax.fori_loop(0, tm, prime, 0)

    for parity in (0, 1):
        cur, nxt = parity, 1 - parity

        @pl.when((i % 2 == parity) & (i >= 1) & (i <= n_used))
        def _():
            scatter_wait(cur)

        @pl.when((i % 2 == parity) & (i < n_used))
        def _():
            gather_wait(cur)
            for r in range(tm):
                gather_copy(tok_ref[(i + 1) * tm + r], r, nxt).start()
            for r in range(tm):
                scatter_copy(dst_ref[i * tm + r], r, nxt).start()

        @pl.when((i % 2 == parity) & (i + 1 <= n_used))
        def _():
            compute(cur)

        @pl.when((i % 2 == parity) & (i == n_used))
        def _():
            gather_wait(cur)

            def flush(r, carry):
                scatter_copy(dst_ref[i * tm + r], r, nxt).start()
                return carry

            lax.fori_loop(0, tm, flush, 0)
            scatter_wait(nxt)


def _moe(hp, eid, wgt, counts, w1, w3, w2, layer):
    t, half = hp.shape
    d = 2 * half
    _, n_exp, _, f = w1.shape
    tm = MOE_TILE
    n_assign = 2 * t
    n_tiles = n_assign // tm + n_exp + 1
    n_rows = n_tiles * tm

    e_flat = eid.T.reshape(-1)
    w_flat = wgt.T.reshape(-1)
    order = jnp.argsort(e_flat, stable=True).astype(jnp.int32)
    padded = ((counts + tm - 1) // tm) * tm
    pad_end = jnp.cumsum(padded)
    pad_start = pad_end - padded
    src_start = jnp.cumsum(counts) - counts
    n_used = (pad_end[-1] // tm).astype(jnp.int32).reshape(1)
    tile_start = jnp.arange(n_tiles, dtype=jnp.int32) * tm
    tile_e = jnp.minimum(jnp.sum((pad_end[None, :] <= tile_start[:, None]).astype(jnp.int32), axis=1), n_exp - 1)
    rows = jnp.arange(n_rows, dtype=jnp.int32)
    j = (rows.reshape(n_tiles, tm) - pad_start[tile_e][:, None]).reshape(-1)
    n_valid = jnp.broadcast_to(counts[tile_e][:, None], (n_tiles, tm)).reshape(-1)
    src0 = jnp.broadcast_to(src_start[tile_e][:, None], (n_tiles, tm)).reshape(-1)
    valid = (j < n_valid) & (rows < pad_end[-1])
    a = order[jnp.clip(src0 + j, 0, n_assign - 1)]
    spare = n_assign + rows % tm
    row_tok = jnp.where(valid, a % t, 0).astype(jnp.int32)
    row_dst = jnp.concatenate([spare[:tm], jnp.where(valid, a, spare)]).astype(jnp.int32)
    row_w = jnp.where(valid, w_flat[a], 0.0).astype(f32).reshape(n_rows, 1)

    return pl.pallas_call(
        functools.partial(_moe_kernel, tm=tm),
        grid_spec=pltpu.PrefetchScalarGridSpec(
            num_scalar_prefetch=4,
            grid=(n_tiles,),
            in_specs=[pl.BlockSpec(memory_space=pl.ANY),
                      pl.BlockSpec((tm, 1), lambda i, te, nu, tok, dst: (i, 0)),
                      pl.BlockSpec((None, None, d, f), lambda i, te, nu, tok, dst: (layer, te[i], 0, 0)),
                      pl.BlockSpec((None, None, d, f), lambda i, te, nu, tok, dst: (layer, te[i], 0, 0)),
                      pl.BlockSpec((None, None, f, d), lambda i, te, nu, tok, dst: (layer, te[i], 0, 0))],
            out_specs=pl.BlockSpec(memory_space=pl.ANY),
            scratch_shapes=[pltpu.VMEM((tm, half), jnp.uint32), pltpu.VMEM((tm, half), jnp.uint32),
                            pltpu.VMEM((tm, d), f32), pltpu.VMEM((tm, d), f32),
                            pltpu.SemaphoreType.DMA((2,)), pltpu.SemaphoreType.DMA((2,))],
        ),
        out_shape=jax.ShapeDtypeStruct((n_assign + tm, d), f32),
        compiler_params=_cp("arbitrary"),
        name="moe",
    )(tile_e, n_used, row_tok, row_dst, hp, row_w, w1, w3, w2)


def kernel(x_prompt, x_sample, c, c_ctx, cache_a_k, cache_a_v, cache_b_k, cache_b_v, cache_c_kv, cache_c_kr,
           g_norm1, g_norm2, g_final, w_ada, b_ada, w_in_ab, w_out_ab, lam_q1, lam_k1, lam_q2, lam_k2,
           g_sub_a, rpb_b, w_down_c, g_q_c, g_kv_c, w_uq_c, w_uk_c, w_uv_c, w_out_c,
           w_group_router, b_group_router, w_expert_router, b_expert_router, w1_moe, w3_moe, w2_moe):
    bc, seq, d = x_prompt.shape
    bs, s_lat, _ = x_sample.shape
    past = cache_a_k.shape[2]
    depth = w_ada.shape[0]
    tc, ts = bc * seq, bs * s_lat
    t = tc + ts
    h_a, h_b = cache_a_k.shape[3], cache_b_k.shape[3]
    ab = (h_a + h_b) * HEAD_DIM
    dims = (bc, seq, bs, s_lat, past)
    assert tc % s_lat == 0 and t % past == 0 and s_lat % (2 * ROW_TILE) == 0

    x = jnp.concatenate([x_prompt.reshape(tc, d), x_sample.reshape(ts, d)], axis=0)
    cond = jnp.concatenate([c_ctx[None], c, jnp.zeros((COND_ROWS_PAD - 1 - bs, d), f32)], axis=0)
    mod = _ada_mod(cond, w_ada, b_ada)
    tables = _rope_tables(s_lat)

    n_exp = N_GROUPS * EXPERTS_PER_GROUP
    f = w1_moe.shape[-1]
    w1_all = w1_moe.reshape(depth, n_exp, d, f).astype(bf16)
    w3_all = w3_moe.reshape(depth, n_exp, d, f).astype(bf16)
    w2_all = w2_moe.reshape(depth, n_exp, f, d).astype(bf16)
    sak, sav, sbk, sbv, sckv, sckr = [], [], [], [], [], []
    moe_pending = None
    for l in range(depth):
        mod4 = mod[l].reshape(COND_ROWS_PAD, 6, 1, d)
        if moe_pending is None:
            h = _norm(x, g_norm1[l], mod4, tc, s_lat, sel=0)[0]
        else:
            x, h = _norm(x, g_norm1[l], mod4, tc, s_lat, sel=0, moe=moe_pending)
        if l % 2 == 0:
            e = l // 2
            lam_init = 0.8 - 0.6 * math.exp(-0.3 * l)
            lam = (jnp.exp(jnp.sum((lam_q1[e] * lam_k1[e]).astype(f32)))
                   - jnp.exp(jnp.sum((lam_q2[e] * lam_k2[e]).astype(f32))) + lam_init)
            p = _matmul(h, w_in_ab[e].astype(bf16), f32)
            wa = h_a * HEAD_DIM
            sak.append(p[:tc, wa:2 * wa].reshape(bc, seq, h_a, HEAD_DIM))
            sav.append(p[:tc, 2 * wa:3 * wa].reshape(bc, seq, h_a, HEAD_DIM))
            wb = h_b * HEAD_DIM
            sbk.append(p[:tc, 3 * wa + wb:3 * wa + 2 * wb].reshape(bc, seq, h_b, HEAD_DIM))
            sbv.append(p[:tc, 3 * wa + 2 * wb:].reshape(bc, seq, h_b, HEAD_DIM))
            o = _attn_even(p, ab, lam, lam_init, g_sub_a[e], rpb_b[e],
                           cache_a_k[:, e].reshape(bs, past, wa), cache_a_v[:, e].reshape(bs, past, wa),
                           cache_b_k[:, e].reshape(bs, past, wb), cache_b_v[:, e].reshape(bs, past, wb),
                           dims, tables)
            w_out = w_out_ab[e].astype(bf16)
        else:
            oi = l // 2
            q_lora, kv_lora = g_q_c.shape[1], g_kv_c.shape[1]
            h_c = w_uk_c.shape[2] // NOPE_DIM
            wd = w_down_c[oi]
            wd = jnp.concatenate([wd, wd[:, q_lora + kv_lora:]], axis=1).astype(bf16)
            cq, ckv, krd = _mla_down(h, wd, g_q_c[oi], g_kv_c[oi])
            sckv.append(ckv[:tc].reshape(bc, seq, kv_lora))
            sckr.append(krd[:tc, :ROPE_DIM].reshape(bc, seq, ROPE_DIM))
            wq = w_uq_c[oi].reshape(q_lora, h_c, NOPE_DIM + ROPE_DIM)
            wq = jnp.concatenate([wq[:, :, :NOPE_DIM].reshape(q_lora, -1), wq[:, :, NOPE_DIM:].reshape(q_lora, -1)],
                                 axis=1).astype(bf16)
            q = _matmul(cq, wq, f32)
            ckv_all = jnp.concatenate([ckv, cache_c_kv[:, oi].reshape(bs * past, kv_lora)], axis=0).astype(bf16)
            wkv = jnp.concatenate([w_uk_c[oi], w_uv_c[oi]], axis=1).astype(bf16)
            kv = _matmul(ckv_all, wkv, bf16)
            krc = cache_c_kr[:, oi]
            o = _attn_mla(q, kv, krd, jnp.concatenate([krc, krc], axis=-1), h_c, dims, tables)
            w_out = w_out_c[oi].astype(bf16)
        x = _matmul(o, w_out, f32, residual=(x, mod4, 2, tc, s_lat))
        w_r = jnp.concatenate([w_group_router[l], w_expert_router[l]], axis=1)
        w_r = jnp.pad(w_r, ((0, 0), (0, LANES - w_r.shape[1]))).astype(bf16)
        b_r = jnp.pad(jnp.concatenate([b_group_router[l], b_expert_router[l]]), (0, LANES - N_GROUPS - n_exp))
        hp, eid, wgt, cnt = _norm(x, g_norm2[l], mod4, tc, s_lat, sel=1,
                                  route=(w_r, b_r.reshape(1, LANES).astype(f32)))
        counts = jnp.sum(cnt, axis=(0, 1))[N_GROUPS:N_GROUPS + n_exp]
        y = _moe(hp, eid[:, :2], wgt[:, :2], counts, w1_all, w3_all, w2_all, l)
        moe_pending = (y, mod4, 5)
    y_prompt = _norm(x, g_final, None, tc, s_lat, moe=moe_pending, final=True, rows=(0, tc))[0].reshape(bc, seq, d)
    y_sample = _norm(x, g_final, None, tc, s_lat, moe=moe_pending, final=True, rows=(tc, ts))[0].reshape(bs, s_lat, d)
    return (y_prompt, y_sample, jnp.stack(sak, axis=1), jnp.stack(sav, axis=1), jnp.stack(sbk, axis=1),
            jnp.stack(sbv, axis=1), jnp.stack(sckv, axis=1), jnp.stack(sckr, axis=1))
```
